```python
import jax, jax.numpy as jnp
from jax import lax
import numpy as np

D_MODEL = 1024
BATCH = 16
SEQ = 2048
DEPTH = 4

N_MIXERS = 2
N_ATTN_LAYERS = (DEPTH + N_MIXERS - 1) // N_MIXERS
N_GMLP_LAYERS = DEPTH // N_MIXERS
HEAD_DIM = 64
N_HEADS = D_MODEL // HEAD_DIM
N_KV_HEADS = N_HEADS // 4
KV_GROUP = N_HEADS // N_KV_HEADS
Q_WIDTH = N_HEADS * HEAD_DIM
KV_WIDTH = N_KV_HEADS * HEAD_DIM
QKV_WIDTH = Q_WIDTH + 2 * KV_WIDTH
Q_BLOCK = 128
ROPE_THETA = 10000.0
ROPE_PAIRS_AXIS = HEAD_DIM // 4
GRID_W = 64
GMLP_FFN = 6 * D_MODEL
GMLP_HALF = GMLP_FFN // 2
GMLP_CHUNK = 128
GMLP_GROUPS = 16
GMLP_GROUP_WIDTH = GMLP_HALF // GMLP_GROUPS
MLP_HIDDEN = 4 * D_MODEL
N_MOD = 6
EPS = 1e-6

kernel_name = "hybrid_gqa_gmlp_adaln_encoder"


def rmsnorm(x, g):
    xf = x.astype(jnp.float32)
    y = xf * lax.rsqrt(jnp.mean(xf * xf, axis=-1, keepdims=True) + EPS)
    return (y * g).astype(x.dtype)


def layernorm(x, g, b):
    xf = x.astype(jnp.float32)
    mu = jnp.mean(xf, axis=-1, keepdims=True)
    xc = xf - mu
    y = xc * lax.rsqrt(jnp.mean(xc * xc, axis=-1, keepdims=True) + EPS)
    return (y * g + b).astype(x.dtype)


def modulate(h, shift, scale):
    return h * (1.0 + scale[:, None, :]) + shift[:, None, :]


def axial_rope_tables(seq_len):
    t = jnp.arange(seq_len, dtype=jnp.int32)
    rows = seq_len // GRID_W
    row = (t // GRID_W - rows // 2).astype(jnp.float32)
    col = (t % GRID_W - GRID_W // 2).astype(jnp.float32)
    inv_freq = ROPE_THETA ** (-jnp.arange(ROPE_PAIRS_AXIS, dtype=jnp.float32) / ROPE_PAIRS_AXIS)
    ang = jnp.concatenate([row[:, None] * inv_freq, col[:, None] * inv_freq], axis=-1)
    return jnp.cos(ang), jnp.sin(ang)


def apply_rope(x, cos, sin):
    xp = x.astype(jnp.float32).reshape(*x.shape[:-1], HEAD_DIM // 2, 2)
    x1, x2 = xp[..., 0], xp[..., 1]
    c, s = cos[:, None, :], sin[:, None, :]
    out = jnp.stack([x1 * c - x2 * s, x1 * s + x2 * c], axis=-1)
    return out.reshape(x.shape).astype(x.dtype)


def gqa_attention(h, w_qkv, q_g, k_g, w_o, cos, sin):
    B, T, _ = h.shape
    qkv = h @ w_qkv
    q, k, v = jnp.split(qkv, [Q_WIDTH, Q_WIDTH + KV_WIDTH], axis=-1)
    q = apply_rope(rmsnorm(q.reshape(B, T, N_HEADS, HEAD_DIM), q_g), cos, sin)
    k = apply_rope(rmsnorm(k.reshape(B, T, N_KV_HEADS, HEAD_DIM), k_g), cos, sin)
    v = v.reshape(B, T, N_KV_HEADS, HEAD_DIM)
    n_blocks = T // Q_BLOCK
    qb = q.reshape(B, n_blocks, Q_BLOCK, N_KV_HEADS, KV_GROUP, HEAD_DIM).transpose(1, 0, 2, 3, 4, 5)
    scale = HEAD_DIM ** -0.5

    def one_block(q_blk):
        s = jnp.einsum('bqkgd,bskd->bkgqs', q_blk, k).astype(jnp.float32) * scale
        p = jax.nn.softmax(s, axis=-1).astype(v.dtype)
        return jnp.einsum('bkgqs,bskd->bqkgd', p, v)

    o = lax.map(one_block, qb)
    o = o.transpose(1, 0, 2, 3, 4, 5).reshape(B, T, Q_WIDTH)
    return o @ w_o


def chunked_gmlp(h, w_in, b_in, ln_g, ln_b, w_s, b_s, w_out):
    B, T, _ = h.shape
    z = jax.nn.gelu(h @ w_in + b_in, approximate=False)
    u, v = jnp.split(z, 2, axis=-1)
    v = layernorm(v, ln_g, ln_b)
    v = v.reshape(B, T // GMLP_CHUNK, GMLP_CHUNK, GMLP_GROUPS, GMLP_GROUP_WIDTH)
    v = jnp.einsum('gpq,bnqgc->bnpgc', w_s, v) + b_s.T[None, None, :, :, None]
    v = v.reshape(B, T, GMLP_HALF)
    return (u * v) @ w_out


def sq_relu_mlp(h, w_in, w_out):
    return jnp.square(jax.nn.relu(h @ w_in)) @ w_out


def setup_inputs(seed: int = 0) -> dict:
    key = jax.random.key(seed)
    ks = jax.random.split(key, 20)

    def nrm(k, shape, scale):
        return jax.random.normal(k, shape, jnp.float32) * scale

    return {
        "x": nrm(ks[0], (BATCH, SEQ, D_MODEL), 1.0),
        "c": nrm(ks[1], (BATCH, D_MODEL), 1.0),
        "ada_w": nrm(ks[2], (DEPTH, D_MODEL, N_MOD * D_MODEL), 0.5 * D_MODEL ** -0.5),
        "ada_b": nrm(ks[3], (DEPTH, N_MOD * D_MODEL), 0.01),
        "norm1_g": 1.0 + nrm(ks[4], (DEPTH, D_MODEL), 0.05),
        "norm2_g": 1.0 + nrm(ks[5], (DEPTH, D_MODEL), 0.05),
        "attn_w_qkv": nrm(ks[6], (N_ATTN_LAYERS, D_MODEL, QKV_WIDTH), D_MODEL ** -0.5),
        "attn_q_norm_g": 1.0 + nrm(ks[7], (N_ATTN_LAYERS, HEAD_DIM), 0.05),
        "attn_k_norm_g": 1.0 + nrm(ks[8], (N_ATTN_LAYERS, HEAD_DIM), 0.05),
        "attn_w_o": nrm(ks[9], (N_ATTN_LAYERS, Q_WIDTH, D_MODEL), Q_WIDTH ** -0.5),
        "gmlp_w_in": nrm(ks[10], (N_GMLP_LAYERS, D_MODEL, GMLP_FFN), D_MODEL ** -0.5),
        "gmlp_b_in": nrm(ks[11], (N_GMLP_LAYERS, GMLP_FFN), 0.01),
        "gmlp_ln_g": 1.0 + nrm(ks[12], (N_GMLP_LAYERS, GMLP_HALF), 0.05),
        "gmlp_ln_b": nrm(ks[13], (N_GMLP_LAYERS, GMLP_HALF), 0.01),
        "gmlp_w_s": nrm(ks[14], (N_GMLP_LAYERS, GMLP_GROUPS, GMLP_CHUNK, GMLP_CHUNK), GMLP_CHUNK ** -0.5),
        "gmlp_b_s": 1.0 + nrm(ks[15], (N_GMLP_LAYERS, GMLP_GROUPS, GMLP_CHUNK), 0.05),
        "gmlp_w_out": nrm(ks[16], (N_GMLP_LAYERS, GMLP_HALF, D_MODEL), GMLP_HALF ** -0.5),
        "mlp_w_in": nrm(ks[17], (DEPTH, D_MODEL, MLP_HIDDEN), D_MODEL ** -0.5),
        "mlp_w_out": nrm(ks[18], (DEPTH, MLP_HIDDEN, D_MODEL), MLP_HIDDEN ** -0.5),
    }


def reference(x, c, ada_w, ada_b, norm1_g, norm2_g, attn_w_qkv, attn_q_norm_g,
              attn_k_norm_g, attn_w_o, gmlp_w_in, gmlp_b_in, gmlp_ln_g, gmlp_ln_b,
              gmlp_w_s, gmlp_b_s, gmlp_w_out, mlp_w_in, mlp_w_out):
    cond = jax.nn.silu(c)
    cos, sin = axial_rope_tables(x.shape[1])
    for i in range(DEPTH):
        mod = cond @ ada_w[i] + ada_b[i]
        sh1, sc1, g1, sh2, sc2, g2 = jnp.split(mod, N_MOD, axis=-1)
        h = modulate(rmsnorm(x, norm1_g[i]), sh1, sc1)
        j = i // N_MIXERS
        if i % N_MIXERS == 0:
            y = gqa_attention(h, attn_w_qkv[j], attn_q_norm_g[j], attn_k_norm_g[j],
                              attn_w_o[j], cos, sin)
        else:
            y = chunked_gmlp(h, gmlp_w_in[j], gmlp_b_in[j], gmlp_ln_g[j], gmlp_ln_b[j],
                             gmlp_w_s[j], gmlp_b_s[j], gmlp_w_out[j])
        x = x + g1[:, None, :] * y
        h = modulate(rmsnorm(x, norm2_g[i]), sh2, sc2)
        x = x + g2[:, None, :] * sq_relu_mlp(h, mlp_w_in[i], mlp_w_out[i])
    return x
```

```python
import functools

import numpy as np
import jax
import jax.numpy as jnp
from jax import lax
from jax.experimental import pallas as pl
from jax.experimental.pallas import tpu as pltpu

D_MODEL = 1024
DEPTH = 4
HEAD_DIM = 64
N_HEADS = 16
N_KV_HEADS = 4
KV_GROUP = N_HEADS // N_KV_HEADS
Q_WIDTH = N_HEADS * HEAD_DIM
KV_WIDTH = N_KV_HEADS * HEAD_DIM
QK_WIDTH = Q_WIDTH + KV_WIDTH
QKV_WIDTH = Q_WIDTH + 2 * KV_WIDTH
ROPE_THETA = 10000.0
ROPE_PAIRS_AXIS = HEAD_DIM // 4
GRID_W = 64
GMLP_FFN = 6 * D_MODEL
GMLP_HALF = GMLP_FFN // 2
GMLP_CHUNK = 128
GMLP_GROUPS = 16
GMLP_GROUP_WIDTH = GMLP_HALF // GMLP_GROUPS
MLP_HIDDEN = 4 * D_MODEL
N_MOD = 6
EPS = 1e-6

LANES = 128
MXU_COLS = 256
VMEM_LIMIT = 56 * 1024 * 1024

F32 = jnp.float32
BF16 = jnp.bfloat16


def _cparams(n_axes):
    return pltpu.CompilerParams(
        dimension_semantics=("arbitrary",) * n_axes,
        vmem_limit_bytes=VMEM_LIMIT,
    )


def _resident(block_shape, index_map):
    return pl.BlockSpec(block_shape, index_map, pipeline_mode=pl.Buffered(1))


def _split_bf16(a):
    hi = a.astype(BF16)
    lo = (a - hi.astype(F32)).astype(BF16)
    return hi, lo


def _modulated_rmsnorm(xv, gain, shift, scale):
    ms = jnp.mean(xv * xv, axis=-1, keepdims=True)
    y = xv * lax.rsqrt(ms + EPS) * gain
    return y * (1.0 + scale) + shift


MOD_TN = 1536


def _mod_kernel(c_ref, w_ref, b_ref, o_ref):
    cv = c_ref[...]
    cond = cv * jax.nn.sigmoid(cv)
    c_hi, c_lo = _split_bf16(cond)
    w_hi, w_lo = _split_bf16(w_ref[0])
    acc = jnp.dot(c_hi, w_hi, preferred_element_type=F32)
    acc += jnp.dot(c_lo, w_hi, preferred_element_type=F32)
    acc += jnp.dot(c_hi, w_lo, preferred_element_type=F32)
    o_ref[0] = acc + b_ref[0]


def _mod_call(c, ada_w, ada_b):
    batch = c.shape[0]
    n_out = N_MOD * D_MODEL
    return pl.pallas_call(
        _mod_kernel,
        grid=(DEPTH, n_out // MOD_TN),
        in_specs=[
            pl.BlockSpec((batch, D_MODEL), lambda l, j: (0, 0)),
            pl.BlockSpec((1, D_MODEL, MOD_TN), lambda l, j: (l, 0, j)),
            pl.BlockSpec((1, 1, MOD_TN), lambda l, j: (l, 0, j)),
        ],
        out_specs=pl.BlockSpec((1, batch, MOD_TN), lambda l, j: (l, 0, j)),
        out_shape=jax.ShapeDtypeStruct((DEPTH, batch, n_out), F32),
        compiler_params=_cparams(2),
        name="adaln_mod",
    )(c, ada_w, ada_b.reshape(DEPTH, 1, n_out))


QKV_TM = 512


def _qkv_kernel(x_ref, mod_ref, g_ref, w_ref, gsum_ref, qkg_ref, cos_ref, sin_ref,
                q_ref, k_ref, v_ref):
    mod = mod_ref[0]
    shift = mod[:, 0:D_MODEL]
    scale = mod[:, D_MODEL:2 * D_MODEL]
    h = _modulated_rmsnorm(x_ref[0], g_ref[...], shift, scale).astype(BF16)
    acc = jnp.dot(h, w_ref[...], preferred_element_type=F32)
    v_ref[0] = acc[:, QK_WIDTH:].astype(BF16)

    tm = acc.shape[0]
    cosv = cos_ref[...]
    sinv = sin_ref[...]
    lane = lax.broadcasted_iota(jnp.int32, (tm, LANES), 1)
    first_half = (lane & (HEAD_DIM // 2)) == 0
    gains = qkg_ref[...]
    for j in range(QK_WIDTH // MXU_COLS):
        y = acc[:, j * MXU_COLS:(j + 1) * MXU_COLS]
        ss = jnp.dot((y * y).astype(BF16), gsum_ref[...], preferred_element_type=F32)
        yn = y * lax.rsqrt(ss * (1.0 / HEAD_DIM) + EPS) * gains[:, j * MXU_COLS:(j + 1) * MXU_COLS]
        for t in range(MXU_COLS // LANES):
            ys = yn[:, t * LANES:(t + 1) * LANES]
            partner = jnp.where(first_half,
                                pltpu.roll(ys, LANES - HEAD_DIM // 2, 1),
                                pltpu.roll(ys, HEAD_DIM // 2, 1))
            r = (ys * cosv + partner * sinv).astype(BF16)
            col = j * MXU_COLS + t * LANES
            if col < Q_WIDTH:
                q_ref[0, :, col:col + LANES] = r
            else:
                k_ref[0, :, col - Q_WIDTH:col - Q_WIDTH + LANES] = r


def _qkv_call(x, mod_l, norm_g, w_qkv, gsum, qk_gain, cos_t, sin_t):
    batch, seq, _ = x.shape
    tm = QKV_TM
    return pl.pallas_call(
        _qkv_kernel,
        grid=(batch, seq // tm),
        in_specs=[
            pl.BlockSpec((1, tm, D_MODEL), lambda b, i: (b, i, 0)),
            pl.BlockSpec((1, 1, N_MOD * D_MODEL), lambda b, i: (b, 0, 0)),
            _resident((1, D_MODEL), lambda b, i: (0, 0)),
            _resident((D_MODEL, QKV_WIDTH), lambda b, i: (0, 0)),
            _resident((MXU_COLS, MXU_COLS), lambda b, i: (0, 0)),
            _resident((1, QK_WIDTH), lambda b, i: (0, 0)),
            pl.BlockSpec((tm, LANES), lambda b, i: (i, 0)),
            pl.BlockSpec((tm, LANES), lambda b, i: (i, 0)),
        ],
        out_specs=[
            pl.BlockSpec((1, tm, Q_WIDTH), lambda b, i: (b, i, 0)),
            pl.BlockSpec((1, tm, KV_WIDTH), lambda b, i: (b, i, 0)),
            pl.BlockSpec((1, tm, KV_WIDTH), lambda b, i: (b, i, 0)),
        ],
        out_shape=[
            jax.ShapeDtypeStruct((batch, seq, Q_WIDTH), BF16),
            jax.ShapeDtypeStruct((batch, seq, KV_WIDTH), BF16),
            jax.ShapeDtypeStruct((batch, seq, KV_WIDTH), BF16),
        ],
        compiler_params=_cparams(2),
        name="attn_qkv",
    )(x, mod_l, norm_g, w_qkv, gsum, qk_gain, cos_t, sin_t)


ATTN_TQ = 512
KV_PER_STEP = LANES // HEAD_DIM


def _attn_kernel(q_ref, k_ref, v_ref, o_ref):
    for g in range(KV_PER_STEP):
        kg = k_ref[0, :, g * HEAD_DIM:(g + 1) * HEAD_DIM]
        vg = v_ref[0, :, g * HEAD_DIM:(g + 1) * HEAD_DIM]
        for hh in range(KV_GROUP):
            h = g * KV_GROUP + hh
            qh = q_ref[0, :, h * HEAD_DIM:(h + 1) * HEAD_DIM]
            s = lax.dot_general(qh, kg, (((1,), (1,)), ((), ())),
                                preferred_element_type=F32)
            m = jnp.max(s, axis=-1, keepdims=True)
            p = jnp.exp(s - m)
            denom = jnp.sum(p, axis=-1, keepdims=True)
            o = jnp.dot(p.astype(BF16), vg, preferred_element_type=F32)
            o_ref[0, :, h * HEAD_DIM:(h + 1) * HEAD_DIM] = (o / denom).astype(BF16)


def _attn_call(q, k, v):
    batch, seq, _ = q.shape
    tq = ATTN_TQ
    qw = KV_PER_STEP * KV_GROUP * HEAD_DIM
    return pl.pallas_call(
        _attn_kernel,
        grid=(batch, N_KV_HEADS // KV_PER_STEP, seq // tq),
        in_specs=[
            pl.BlockSpec((1, tq, qw), lambda b, j, i: (b, i, j)),
            pl.BlockSpec((1, seq, LANES), lambda b, j, i: (b, 0, j)),
            pl.BlockSpec((1, seq, LANES), lambda b, j, i: (b, 0, j)),
        ],
        out_specs=pl.BlockSpec((1, tq, qw), lambda b, j, i: (b, i, j)),
        out_shape=jax.ShapeDtypeStruct((batch, seq, Q_WIDTH), BF16),
        compiler_params=_cparams(3),
        name="attn_core",
    )(q, k, v)


PROJ_TM = 512


def _proj_kernel(x_ref, o_ref, mod_ref, w_ref, out_ref):
    gate = mod_ref[0][:, 2 * D_MODEL:3 * D_MODEL]
    y = jnp.dot(o_ref[0], w_ref[...], preferred_element_type=F32)
    out_ref[0] = x_ref[0] + gate * y


def _proj_call(x, o, mod_l, w_o):
    batch, seq, _ = x.shape
    tm = PROJ_TM
    return pl.pallas_call(
        _proj_kernel,
        grid=(batch, seq // tm),
        in_specs=[
            pl.BlockSpec((1, tm, D_MODEL), lambda b, i: (b, i, 0)),
            pl.BlockSpec((1, tm, Q_WIDTH), lambda b, i: (b, i, 0)),
            pl.BlockSpec((1, 1, N_MOD * D_MODEL), lambda b, i: (b, 0, 0)),
            _resident((Q_WIDTH, D_MODEL), lambda b, i: (0, 0)),
        ],
        out_specs=pl.BlockSpec((1, tm, D_MODEL), lambda b, i: (b, i, 0)),
        out_shape=jax.ShapeDtypeStruct(x.shape, F32),
        compiler_params=_cparams(2),
        name="attn_proj",
    )(x, o, mod_l, w_o)


MLP_TM = 512
MLP_TH = 1024


def _mlp_kernel(x_ref, mod_ref, g_ref, win_ref, wout_ref, out_ref, a_scr):
    mod = mod_ref[0]
    shift = mod[:, 3 * D_MODEL:4 * D_MODEL]
    scale = mod[:, 4 * D_MODEL:5 * D_MODEL]
    gate = mod[:, 5 * D_MODEL:6 * D_MODEL]
    xv = x_ref[0]
    h = _modulated_rmsnorm(xv, g_ref[...], shift, scale).astype(BF16)
    for c in range(MLP_HIDDEN // MLP_TH):
        a = jnp.dot(h, win_ref[:, c * MLP_TH:(c + 1) * MLP_TH], preferred_element_type=F32)
        a = jnp.maximum(a, 0.0)
        a_scr[:, c * MLP_TH:(c + 1) * MLP_TH] = (a * a).astype(BF16)
    y = jnp.dot(a_scr[...], wout_ref[...], preferred_element_type=F32)
    out_ref[0] = xv + gate * y


def _mlp_call(x, mod_l, norm_g, w_in, w_out):
    batch, seq, _ = x.shape
    tm = MLP_TM
    return pl.pallas_call(
        _mlp_kernel,
        grid=(batch, seq // tm),
        in_specs=[
            pl.BlockSpec((1, tm, D_MODEL), lambda b, i: (b, i, 0)),
            pl.BlockSpec((1, 1, N_MOD * D_MODEL), lambda b, i: (b, 0, 0)),
            _resident((1, D_MODEL), lambda b, i: (0, 0)),
            _resident((D_MODEL, MLP_HIDDEN), lambda b, i: (0, 0)),
            _resident((MLP_HIDDEN, D_MODEL), lambda b, i: (0, 0)),
        ],
        out_specs=pl.BlockSpec((1, tm, D_MODEL), lambda b, i: (b, i, 0)),
        out_shape=jax.ShapeDtypeStruct(x.shape, F32),
        scratch_shapes=[pltpu.VMEM((tm, MLP_HIDDEN), BF16)],
        compiler_params=_cparams(2),
        name="sq_relu_mlp",
    )(x, mod_l, norm_g, w_in, w_out)


GMLP_TM = 256
GMLP_TN = 1024
PAIR_WIDTH = 2 * GMLP_GROUP_WIDTH


def _gelu(z):
    return 0.5 * z * (1.0 + lax.erf(z * (2.0 ** -0.5)))


def _gmlp_kernel(x_ref, mod_ref, g_ref, win_ref, bin_ref, lng_ref, lnb_ref, ws_ref, bs_ref,
                 wout_ref, out_ref, v_scr, vn_scr, uv_scr):
    mod = mod_ref[0]
    shift = mod[:, 0:D_MODEL]
    scale = mod[:, D_MODEL:2 * D_MODEL]
    gate = mod[:, 2 * D_MODEL:3 * D_MODEL]
    xv = x_ref[0]
    tm = xv.shape[0]
    h = _modulated_rmsnorm(xv, g_ref[...], shift, scale).astype(BF16)
    n_col = GMLP_HALF // GMLP_TN

    s1 = jnp.zeros((tm, 1), F32)
    for c in range(n_col):
        lo = GMLP_HALF + c * GMLP_TN
        z = jnp.dot(h, win_ref[:, lo:lo + GMLP_TN], preferred_element_type=F32)
        zg = _gelu(z + bin_ref[:, lo:lo + GMLP_TN])
        v_scr[:, c * GMLP_TN:(c + 1) * GMLP_TN] = zg
        s1 += jnp.sum(zg, axis=-1, keepdims=True)
    mu = s1 * (1.0 / GMLP_HALF)
    s2 = jnp.zeros((tm, 1), F32)
    for c in range(n_col):
        xc = v_scr[:, c * GMLP_TN:(c + 1) * GMLP_TN] - mu
        s2 += jnp.sum(xc * xc, axis=-1, keepdims=True)
    rstd = lax.rsqrt(s2 * (1.0 / GMLP_HALF) + EPS)
    for c in range(n_col):
        sl = slice(c * GMLP_TN, (c + 1) * GMLP_TN)
        vn = (v_scr[:, sl] - mu) * rstd * lng_ref[:, sl] + lnb_ref[:, sl]
        vn_scr[:, sl] = vn.astype(BF16)

    lane = lax.broadcasted_iota(jnp.int32, (GMLP_CHUNK, LANES), 1)
    low_lanes = lane < (GMLP_GROUP_WIDTH - LANES)
    for n in range(tm // GMLP_CHUNK):
        rows = slice(n * GMLP_CHUNK, (n + 1) * GMLP_CHUNK)
        for j in range(GMLP_GROUPS // 2):
            base = j * PAIR_WIDTH
            va = vn_scr[rows, base:base + LANES]
            vb = vn_scr[rows, base + LANES:base + 2 * LANES]
            vc = vn_scr[rows, base + 2 * LANES:base + 3 * LANES]
            ra = jnp.dot(ws_ref[2 * j], jnp.concatenate([va, vc], axis=1),
                         preferred_element_type=F32)
            rb = jnp.dot(ws_ref[2 * j + 1], jnp.concatenate([vb, vc], axis=1),
                         preferred_element_type=F32)
            rc = jnp.where(low_lanes, ra[:, LANES:], rb[:, LANES:])
            v_scr[rows, base:base + LANES] = ra[:, :LANES] + bs_ref[:, base:base + LANES]
            v_scr[rows, base + LANES:base + 2 * LANES] = (
                rb[:, :LANES] + bs_ref[:, base + LANES:base + 2 * LANES])
            v_scr[rows, base + 2 * LANES:base + 3 * LANES] = (
                rc + bs_ref[:, base + 2 * LANES:base + 3 * LANES])

    for c in range(n_col):
        sl = slice(c * GMLP_TN, (c + 1) * GMLP_TN)
        z = jnp.dot(h, win_ref[:, sl], preferred_element_type=F32)
        u = _gelu(z + bin_ref[:, sl])
        uv_scr[:, sl] = (u * v_scr[:, sl]).astype(BF16)
    y = jnp.dot(uv_scr[...], wout_ref[...], preferred_element_type=F32)
    out_ref[0] = xv + gate * y


def _gmlp_call(x, mod_l, norm_g, w_in, b_in, ln_g, ln_b, w_s, bs_table, w_out):
    batch, seq, _ = x.shape
    tm = GMLP_TM
    return pl.pallas_call(
        _gmlp_kernel,
        grid=(batch, seq // tm),
        in_specs=[
            pl.BlockSpec((1, tm, D_MODEL), lambda b, i: (b, i, 0)),
            pl.BlockSpec((1, 1, N_MOD * D_MODEL), lambda b, i: (b, 0, 0)),
            _resident((1, D_MODEL), lambda b, i: (0, 0)),
            _resident((D_MODEL, GMLP_FFN), lambda b, i: (0, 0)),
            _resident((1, GMLP_FFN), lambda b, i: (0, 0)),
            _resident((1, GMLP_HALF), lambda b, i: (0, 0)),
            _resident((1, GMLP_HALF), lambda b, i: (0, 0)),
            _resident((GMLP_GROUPS, GMLP_CHUNK, GMLP_CHUNK), lambda b, i: (0, 0, 0)),
            _resident((GMLP_CHUNK, GMLP_HALF), lambda b, i: (0, 0)),
            _resident((GMLP_HALF, D_MODEL), lambda b, i: (0, 0)),
        ],
        out_specs=pl.BlockSpec((1, tm, D_MODEL), lambda b, i: (b, i, 0)),
        out_shape=jax.ShapeDtypeStruct(x.shape, F32),
        scratch_shapes=[
            pltpu.VMEM((tm, GMLP_HALF), F32),
            pltpu.VMEM((tm, GMLP_HALF), BF16),
            pltpu.VMEM((tm, GMLP_HALF), BF16),
        ],
        compiler_params=_cparams(2),
        name="gmlp_mixer",
    )(x, mod_l, norm_g, w_in, b_in, ln_g, ln_b, w_s, bs_table, w_out)


def _rope_perm():
    return np.concatenate([np.arange(0, HEAD_DIM, 2), np.arange(1, HEAD_DIM, 2)])


def _qkv_column_perm():
    p = _rope_perm()
    cols = [h * HEAD_DIM + p for h in range(N_HEADS + N_KV_HEADS)]
    cols.append(np.arange(QK_WIDTH, QKV_WIDTH))
    return np.concatenate(cols)


def _gmlp_half_perm():
    cols = []
    for j in range(GMLP_GROUPS // 2):
        a = 2 * j * GMLP_GROUP_WIDTH + np.arange(GMLP_GROUP_WIDTH)
        b = a + GMLP_GROUP_WIDTH
        cols += [a[:LANES], b[:LANES], a[LANES:], b[LANES:]]
    return np.concatenate(cols)


def _rope_tables(seq_len):
    t = jnp.arange(seq_len, dtype=jnp.int32)
    rows = seq_len // GRID_W
    row = (t // GRID_W - rows // 2).astype(F32)
    col = (t % GRID_W - GRID_W // 2).astype(F32)
    inv_freq = ROPE_THETA ** (-jnp.arange(ROPE_PAIRS_AXIS, dtype=F32) / ROPE_PAIRS_AXIS)
    ang = jnp.concatenate([row[:, None] * inv_freq, col[:, None] * inv_freq], axis=-1)
    cos, sin = jnp.cos(ang), jnp.sin(ang)
    reps = LANES // HEAD_DIM
    cos_t = jnp.tile(jnp.concatenate([cos, cos], axis=-1), (1, reps))
    sin_t = jnp.tile(jnp.concatenate([-sin, sin], axis=-1), (1, reps))
    return cos_t, sin_t


def kernel(x, c, ada_w, ada_b, norm1_g, norm2_g, attn_w_qkv, attn_q_norm_g, attn_k_norm_g,
           attn_w_o, gmlp_w_in, gmlp_b_in, gmlp_ln_g, gmlp_ln_b, gmlp_w_s, gmlp_b_s,
           gmlp_w_out, mlp_w_in, mlp_w_out):
    batch, seq, _ = x.shape
    mod = _mod_call(c, ada_w, ada_b)
    cos_t, sin_t = _rope_tables(seq)

    rope_p = _rope_perm()
    qkv_p = _qkv_column_perm()
    half_p = _gmlp_half_perm()
    full_p = np.concatenate([half_p, GMLP_HALF + half_p])
    group_of = half_p // GMLP_GROUP_WIDTH
    head_ids = np.arange(MXU_COLS) // HEAD_DIM
    gsum = jnp.asarray(head_ids[:, None] == head_ids[None, :], dtype=BF16)

    for i in range(DEPTH):
        mod_l = mod[i].reshape(batch, 1, N_MOD * D_MODEL)
        j = i // 2
        n1 = norm1_g[i].reshape(1, D_MODEL)
        if i % 2 == 0:
            w_qkv = attn_w_qkv[j][:, qkv_p].astype(BF16)
            qg = attn_q_norm_g[j][rope_p] * (HEAD_DIM ** -0.5)
            kg = attn_k_norm_g[j][rope_p]
            qk_gain = jnp.concatenate([jnp.tile(qg, N_HEADS), jnp.tile(kg, N_KV_HEADS)])
            q, k, v = _qkv_call(x, mod_l, n1, w_qkv, gsum, qk_gain.reshape(1, QK_WIDTH),
                                cos_t, sin_t)
            o = _attn_call(q, k, v)
            x = _proj_call(x, o, mod_l, attn_w_o[j].astype(BF16))
        else:
            w_in = gmlp_w_in[j][:, full_p].astype(BF16)
            b_in = gmlp_b_in[j][full_p].reshape(1, GMLP_FFN)
            ln_g = gmlp_ln_g[j][half_p].reshape(1, GMLP_HALF)
            ln_b = gmlp_ln_b[j][half_p].reshape(1, GMLP_HALF)
            bs_table = gmlp_b_s[j].T[:, group_of]
            w_out = gmlp_w_out[j][half_p, :].astype(BF16)
            x = _gmlp_call(x, mod_l, n1, w_in, b_in, ln_g, ln_b, gmlp_w_s[j].astype(BF16),
                           bs_table, w_out)
        x = _mlp_call(x, mod_l, norm2_g[i].reshape(1, D_MODEL),
                      mlp_w_in[i].astype(BF16), mlp_w_out[i].astype(BF16))
    return x
```

```python
import functools
import math

import jax
import jax.numpy as jnp
from jax import lax
from jax.experimental import pallas as pl
from jax.experimental.pallas import tpu as pltpu

D_MODEL = 1024
DEPTH = 4
HEAD_DIM = 64
N_HEADS = 16
N_KV_HEADS = 4
KV_GROUP = N_HEADS // N_KV_HEADS
Q_WIDTH = N_HEADS * HEAD_DIM
KV_WIDTH = N_KV_HEADS * HEAD_DIM
QK_WIDTH = Q_WIDTH + KV_WIDTH
QKV_WIDTH = Q_WIDTH + 2 * KV_WIDTH
ROPE_THETA = 10000.0
ROPE_PAIRS_AXIS = HEAD_DIM // 4
GRID_W = 64
GMLP_FFN = 6 * D_MODEL
GMLP_HALF = GMLP_FFN // 2
GMLP_CHUNK = 128
GMLP_GROUPS = 16
GMLP_GROUP_WIDTH = GMLP_HALF // GMLP_GROUPS
MLP_HIDDEN = 4 * D_MODEL
N_MOD = 6
EPS = 1e-6

LANES = 128
BF16_SUBLANES = 16
MXU_COLS = 256
VMEM_LIMIT = 56 * 1024 * 1024

F32 = jnp.float32
BF16 = jnp.bfloat16


def _cparams(n_axes):
    return pltpu.CompilerParams(
        dimension_semantics=("arbitrary",) * n_axes,
        vmem_limit_bytes=VMEM_LIMIT,
    )


def _resident(block_shape, index_map):
    return pl.BlockSpec(block_shape, index_map, pipeline_mode=pl.Buffered(1))


def _split_bf16(a):
    hi = a.astype(BF16)
    lo = (a - hi.astype(F32)).astype(BF16)
    return hi, lo


def _modulated_rmsnorm(xv, gain, shift, scale):
    ms = jnp.mean(xv * xv, axis=-1, keepdims=True)
    y = xv * lax.rsqrt(ms + EPS) * gain
    return y * (1.0 + scale) + shift


MOD_TN = 1536


def _mod_kernel(c_ref, w_ref, b_ref, o_ref):
    cv = c_ref[...]
    cond = cv * jax.nn.sigmoid(cv)
    c_hi, c_lo = _split_bf16(cond)
    w_hi, w_lo = _split_bf16(w_ref[0])
    acc = jnp.dot(c_hi, w_hi, preferred_element_type=F32)
    acc += jnp.dot(c_lo, w_hi, preferred_element_type=F32)
    acc += jnp.dot(c_hi, w_lo, preferred_element_type=F32)
    o_ref[0] = acc + b_ref[0]


def _mod_call(c, ada_w, ada_b):
    batch = c.shape[0]
    n_out = N_MOD * D_MODEL
    return pl.pallas_call(
        _mod_kernel,
        grid=(DEPTH, n_out // MOD_TN),
        in_specs=[
            pl.BlockSpec((batch, D_MODEL), lambda l, j: (0, 0)),
            pl.BlockSpec((1, D_MODEL, MOD_TN), lambda l, j: (l, 0, j)),
            pl.BlockSpec((1, 1, MOD_TN), lambda l, j: (l, 0, j)),
        ],
        out_specs=pl.BlockSpec((1, batch, MOD_TN), lambda l, j: (l, 0, j)),
        out_shape=jax.ShapeDtypeStruct((DEPTH, batch, n_out), F32),
        compiler_params=_cparams(2),
        name="adaln_mod",
    )(c, ada_w, ada_b.reshape(DEPTH, 1, n_out))


QKV_TM = 1024
QKV_SUB = 256


def _qkv_kernel(x_ref, mod_ref, g_ref, w_ref, gsum_ref, qkg_ref, cos_ref, sin_ref,
                qt_ref, k_ref, vt_ref):
    mod = mod_ref[0]
    shift = mod[:, 0:D_MODEL]
    scale = mod[:, D_MODEL:2 * D_MODEL]
    gains = qkg_ref[...]
    lane = lax.broadcasted_iota(jnp.int32, (QKV_SUB, LANES), 1)
    first_half = (lane & (HEAD_DIM // 2)) == 0
    n_sub = x_ref.shape[1] // QKV_SUB

    def project(s):
        rows = slice(s * QKV_SUB, (s + 1) * QKV_SUB)
        h = _modulated_rmsnorm(x_ref[0, rows, :], g_ref[...], shift, scale).astype(BF16)
        return jnp.dot(h, w_ref[...], preferred_element_type=F32)

    def finish(s, acc):
        rows = slice(s * QKV_SUB, (s + 1) * QKV_SUB)
        vt_ref[0, :, rows] = acc[:, QK_WIDTH:].T.astype(BF16)
        cosv = cos_ref[rows, :]
        sinv = sin_ref[rows, :]
        for j in range(QK_WIDTH // MXU_COLS):
            y = acc[:, j * MXU_COLS:(j + 1) * MXU_COLS]
            ss = jnp.dot((y * y).astype(BF16), gsum_ref[...], preferred_element_type=F32)
            yn = (y * lax.rsqrt(ss * (1.0 / HEAD_DIM) + EPS)
                  * gains[:, j * MXU_COLS:(j + 1) * MXU_COLS])
            for t in range(MXU_COLS // LANES):
                ys = yn[:, t * LANES:(t + 1) * LANES]
                partner = jnp.where(first_half,
                                    pltpu.roll(ys, LANES - HEAD_DIM // 2, 1),
                                    pltpu.roll(ys, HEAD_DIM // 2, 1))
                r = ys * cosv + partner * sinv
                col = j * MXU_COLS + t * LANES
                if col < Q_WIDTH:
                    qt_ref[0, col:col + LANES, rows] = r.T.astype(BF16)
                else:
                    k_ref[0, rows, col - Q_WIDTH:col - Q_WIDTH + LANES] = r.astype(BF16)

    acc_next = project(0)
    for s in range(n_sub):
        acc = acc_next
        if s + 1 < n_sub:
            acc_next = project(s + 1)
        finish(s, acc)


def _qkv_call(x, mod_l, norm_g, w_qkv, gsum, qk_gain, cos_t, sin_t):
    batch, seq, _ = x.shape
    tm = QKV_TM
    return pl.pallas_call(
        _qkv_kernel,
        grid=(batch, seq // tm),
        in_specs=[
            pl.BlockSpec((1, tm, D_MODEL), lambda b, i: (b, i, 0)),
            pl.BlockSpec((1, 1, N_MOD * D_MODEL), lambda b, i: (b, 0, 0)),
            _resident((1, D_MODEL), lambda b, i: (0, 0)),
            _resident((D_MODEL, QKV_WIDTH), lambda b, i: (0, 0)),
            _resident((MXU_COLS, MXU_COLS), lambda b, i: (0, 0)),
            _resident((1, QK_WIDTH), lambda b, i: (0, 0)),
            pl.BlockSpec((tm, LANES), lambda b, i: (i, 0)),
            pl.BlockSpec((tm, LANES), lambda b, i: (i, 0)),
        ],
        out_specs=[
            pl.BlockSpec((1, Q_WIDTH, tm), lambda b, i: (b, 0, i)),
            pl.BlockSpec((1, tm, KV_WIDTH), lambda b, i: (b, i, 0)),
            pl.BlockSpec((1, KV_WIDTH, tm), lambda b, i: (b, 0, i)),
        ],
        out_shape=[
            jax.ShapeDtypeStruct((batch, Q_WIDTH, seq), BF16),
            jax.ShapeDtypeStruct((batch, seq, KV_WIDTH), BF16),
            jax.ShapeDtypeStruct((batch, KV_WIDTH, seq), BF16),
        ],
        compiler_params=_cparams(2),
        name="attn_qkv",
    )(x, mod_l, norm_g, w_qkv, gsum, qk_gain, cos_t, sin_t)


ATTN_TQ = 512
ATTN_KC = 256
KV_PER_STEP = LANES // HEAD_DIM


def _attn_kernel(qt_ref, k_ref, vt_ref, o_ref, st_scr, pt_scr):
    kk = k_ref[0]
    seq = kk.shape[0]
    tq = qt_ref.shape[2]
    zeros = jnp.zeros((HEAD_DIM, tq), BF16)
    ones = jnp.ones((BF16_SUBLANES, seq), BF16)
    n_heads = KV_PER_STEP * KV_GROUP

    def scores(h):
        qt = qt_ref[0, h * HEAD_DIM:(h + 1) * HEAD_DIM, :]
        qt_pad = jnp.concatenate([qt, zeros] if h < KV_GROUP else [zeros, qt], axis=0)
        st = jnp.dot(kk, qt_pad, preferred_element_type=F32)
        st_scr[h % 2] = st
        return jnp.max(st, axis=0, keepdims=True)

    m_next = scores(0)
    done = []
    for h in range(n_heads):
        m = m_next
        if h + 1 < n_heads:
            m_next = scores(h + 1)
        g = h // KV_GROUP
        slot = h % 2
        vt_aug = jnp.concatenate([vt_ref[0, g * HEAD_DIM:(g + 1) * HEAD_DIM, :], ones], axis=0)
        for kc in range(seq // ATTN_KC):
            rows = slice(kc * ATTN_KC, (kc + 1) * ATTN_KC)
            pt_scr[slot, rows, :] = jnp.exp2(st_scr[slot, rows, :] - m).astype(BF16)
        ot = jnp.dot(vt_aug, pt_scr[slot], preferred_element_type=F32)
        done.append(ot[:HEAD_DIM] / ot[HEAD_DIM:HEAD_DIM + 1])
        if len(done) == 2:
            both = jnp.concatenate(done, axis=0)
            o_ref[0, :, (h - 1) * HEAD_DIM:(h + 1) * HEAD_DIM] = both.T.astype(BF16)
            done = []


def _attn_call(qt, k, vt):
    batch, seq, _ = k.shape
    tq = ATTN_TQ
    qw = KV_PER_STEP * KV_GROUP * HEAD_DIM
    return pl.pallas_call(
        _attn_kernel,
        grid=(batch, N_KV_HEADS // KV_PER_STEP, seq // tq),
        in_specs=[
            pl.BlockSpec((1, qw, tq), lambda b, j, i: (b, j, i)),
            pl.BlockSpec((1, seq, LANES), lambda b, j, i: (b, 0, j)),
            pl.BlockSpec((1, LANES, seq), lambda b, j, i: (b, j, 0)),
        ],
        out_specs=pl.BlockSpec((1, tq, qw), lambda b, j, i: (b, i, j)),
        out_shape=jax.ShapeDtypeStruct((batch, seq, Q_WIDTH), BF16),
        scratch_shapes=[pltpu.VMEM((2, seq, tq), F32), pltpu.VMEM((2, seq, tq), BF16)],
        compiler_params=_cparams(3),
        name="attn_core",
    )(qt, k, vt)


MLP_TM = 512
MLP_TH = 1024


def _mlp_body(xv, mod, g_ref, win_ref, wout_ref, out_ref, a_scr):
    shift = mod[:, 3 * D_MODEL:4 * D_MODEL]
    scale = mod[:, 4 * D_MODEL:5 * D_MODEL]
    gate = mod[:, 5 * D_MODEL:6 * D_MODEL]
    h = _modulated_rmsnorm(xv, g_ref[...], shift, scale).astype(BF16)
    for c in range(MLP_HIDDEN // MLP_TH):
        a = jnp.dot(h, win_ref[:, c * MLP_TH:(c + 1) * MLP_TH], preferred_element_type=F32)
        a = jnp.maximum(a, 0.0)
        a_scr[:, c * MLP_TH:(c + 1) * MLP_TH] = (a * a).astype(BF16)
    y = jnp.dot(a_scr[...], wout_ref[...], preferred_element_type=F32)
    out_ref[0] = xv + gate * y


def _mlp_kernel(x_ref, mod_ref, g_ref, win_ref, wout_ref, out_ref, a_scr):
    _mlp_body(x_ref[0], mod_ref[0], g_ref, win_ref, wout_ref, out_ref, a_scr)


def _proj_mlp_kernel(x_ref, o_ref, wo_ref, mod_ref, g_ref, win_ref, wout_ref, out_ref, a_scr):
    mod = mod_ref[0]
    gate1 = mod[:, 2 * D_MODEL:3 * D_MODEL]
    y = jnp.dot(o_ref[0], wo_ref[...], preferred_element_type=F32)
    _mlp_body(x_ref[0] + gate1 * y, mod, g_ref, win_ref, wout_ref, out_ref, a_scr)


def _mlp_call(x, mod_l, norm_g, w_in, w_out, attn=None):
    batch, seq, _ = x.shape
    tm = MLP_TM
    tile = pl.BlockSpec((1, tm, D_MODEL), lambda b, i: (b, i, 0))
    specs = [
        pl.BlockSpec((1, 1, N_MOD * D_MODEL), lambda b, i: (b, 0, 0)),
        _resident((1, D_MODEL), lambda b, i: (0, 0)),
        _resident((D_MODEL, MLP_HIDDEN), lambda b, i: (0, 0)),
        _resident((MLP_HIDDEN, D_MODEL), lambda b, i: (0, 0)),
    ]
    args = (mod_l, norm_g, w_in, w_out)
    if attn is None:
        body, pre_specs, pre_args = _mlp_kernel, [tile], (x,)
    else:
        body = _proj_mlp_kernel
        pre_specs = [tile, pl.BlockSpec((1, tm, Q_WIDTH), lambda b, i: (b, i, 0)),
                     _resident((Q_WIDTH, D_MODEL), lambda b, i: (0, 0))]
        pre_args = (x,) + tuple(attn)
    return pl.pallas_call(
        body,
        grid=(batch, seq // tm),
        in_specs=pre_specs + specs,
        out_specs=tile,
        out_shape=jax.ShapeDtypeStruct(x.shape, F32),
        scratch_shapes=[pltpu.VMEM((tm, MLP_HIDDEN), BF16)],
        compiler_params=_cparams(2),
        name="sq_relu_mlp" if attn is None else "proj_sq_relu_mlp",
    )(*pre_args, *args)


GMLP_TM = 512
GMLP_TN = 1024
PAIR_WIDTH = 2 * GMLP_GROUP_WIDTH


def _gelu(z):
    return 0.5 * z * (1.0 + lax.erf(z * (2.0 ** -0.5)))


def _gmlp_kernel(x_ref, mod_ref, g_ref, win_ref, bin_ref, lng_ref, lnb_ref, ws_ref, bs_ref,
                 wout_ref, out_ref, v_scr, u_scr, vn_scr, uv_scr):
    mod = mod_ref[0]
    shift = mod[:, 0:D_MODEL]
    scale = mod[:, D_MODEL:2 * D_MODEL]
    gate = mod[:, 2 * D_MODEL:3 * D_MODEL]
    xv = x_ref[0]
    tm = xv.shape[0]
    h = _modulated_rmsnorm(xv, g_ref[...], shift, scale).astype(BF16)
    n_col = GMLP_HALF // GMLP_TN

    s1 = jnp.zeros((tm, 1), F32)
    for c in range(n_col):
        lo = GMLP_HALF + c * GMLP_TN
        z = jnp.dot(h, win_ref[:, lo:lo + GMLP_TN], preferred_element_type=F32)
        zg = _gelu(z + bin_ref[:, lo:lo + GMLP_TN])
        v_scr[:, c * GMLP_TN:(c + 1) * GMLP_TN] = zg
        s1 += jnp.sum(zg, axis=-1, keepdims=True)
    for c in range(n_col):
        sl = slice(c * GMLP_TN, (c + 1) * GMLP_TN)
        z = jnp.dot(h, win_ref[:, sl], preferred_element_type=F32)
        u_scr[:, sl] = _gelu(z + bin_ref[:, sl])
    mu = s1 * (1.0 / GMLP_HALF)
    s2 = jnp.zeros((tm, 1), F32)
    for c in range(n_col):
        xc = v_scr[:, c * GMLP_TN:(c + 1) * GMLP_TN] - mu
        s2 += jnp.sum(xc * xc, axis=-1, keepdims=True)
    rstd = lax.rsqrt(s2 * (1.0 / GMLP_HALF) + EPS)
    for c in range(n_col):
        sl = slice(c * GMLP_TN, (c + 1) * GMLP_TN)
        vn = (v_scr[:, sl] - mu) * rstd * lng_ref[:, sl] + lnb_ref[:, sl]
        vn_scr[:, sl] = vn.astype(BF16)

    lane = lax.broadcasted_iota(jnp.int32, (GMLP_CHUNK, LANES), 1)
    low_lanes = lane < (GMLP_GROUP_WIDTH - LANES)
    for n in range(tm // GMLP_CHUNK):
        rows = slice(n * GMLP_CHUNK, (n + 1) * GMLP_CHUNK)
        for j in range(GMLP_GROUPS // 2):
            base = j * PAIR_WIDTH
            va = vn_scr[rows, base:base + LANES]
            vb = vn_scr[rows, base + LANES:base + 2 * LANES]
            vc = vn_scr[rows, base + 2 * LANES:base + 3 * LANES]
            ra = jnp.dot(ws_ref[2 * j], jnp.concatenate([va, vc], axis=1),
                         preferred_element_type=F32)
            rb = jnp.dot(ws_ref[2 * j + 1], jnp.concatenate([vb, vc], axis=1),
                         preferred_element_type=F32)
            rc = jnp.where(low_lanes, ra[:, LANES:], rb[:, LANES:])
            v_scr[rows, base:base + LANES] = ra[:, :LANES] + bs_ref[:, base:base + LANES]
            v_scr[rows, base + LANES:base + 2 * LANES] = (
                rb[:, :LANES] + bs_ref[:, base + LANES:base + 2 * LANES])
            v_scr[rows, base + 2 * LANES:base + 3 * LANES] = (
                rc + bs_ref[:, base + 2 * LANES:base + 3 * LANES])

    for c in range(n_col):
        sl = slice(c * GMLP_TN, (c + 1) * GMLP_TN)
        uv_scr[:, sl] = (u_scr[:, sl] * v_scr[:, sl]).astype(BF16)
    y = jnp.dot(uv_scr[...], wout_ref[...], preferred_element_type=F32)
    out_ref[0] = xv + gate * y


def _gmlp_call(x, mod_l, norm_g, w_in, b_in, ln_g, ln_b, w_s, bs_table, w_out):
    batch, seq, _ = x.shape
    tm = GMLP_TM
    return pl.pallas_call(
        _gmlp_kernel,
        grid=(batch, seq // tm),
        in_specs=[
            pl.BlockSpec((1, tm, D_MODEL), lambda b, i: (b, i, 0)),
            pl.BlockSpec((1, 1, N_MOD * D_MODEL), lambda b, i: (b, 0, 0)),
            _resident((1, D_MODEL), lambda b, i: (0, 0)),
            _resident((D_MODEL, GMLP_FFN), lambda b, i: (0, 0)),
            _resident((1, GMLP_FFN), lambda b, i: (0, 0)),
            _resident((1, GMLP_HALF), lambda b, i: (0, 0)),
            _resident((1, GMLP_HALF), lambda b, i: (0, 0)),
            _resident((GMLP_GROUPS, GMLP_CHUNK, GMLP_CHUNK), lambda b, i: (0, 0, 0)),
            _resident((GMLP_CHUNK, GMLP_HALF), lambda b, i: (0, 0)),
            _resident((GMLP_HALF, D_MODEL), lambda b, i: (0, 0)),
        ],
        out_specs=pl.BlockSpec((1, tm, D_MODEL), lambda b, i: (b, i, 0)),
        out_shape=jax.ShapeDtypeStruct(x.shape, F32),
        scratch_shapes=[
            pltpu.VMEM((tm, GMLP_HALF), F32),
            pltpu.VMEM((tm, GMLP_HALF), F32),
            pltpu.VMEM((tm, GMLP_HALF), BF16),
            pltpu.VMEM((tm, GMLP_HALF), BF16),
        ],
        compiler_params=_cparams(2),
        name="gmlp_mixer",
    )(x, mod_l, norm_g, w_in, b_in, ln_g, ln_b, w_s, bs_table, w_out)


def _rope_order(a):
    lead = a.shape[:-1]
    n = a.shape[-1] // HEAD_DIM
    a = a.reshape(lead + (n, HEAD_DIM // 2, 2))
    return jnp.swapaxes(a, -1, -2).reshape(lead + (n * HEAD_DIM,))


def _pair_order(a, axis):
    a = jnp.moveaxis(a, axis, -1)
    lead = a.shape[:-1]
    n_pairs = GMLP_GROUPS // 2
    a = a.reshape(lead + (n_pairs, 2, GMLP_GROUP_WIDTH))
    head = a[..., :LANES].reshape(lead + (n_pairs, 2 * LANES))
    tail = a[..., LANES:].reshape(lead + (n_pairs, PAIR_WIDTH - 2 * LANES))
    out = jnp.concatenate([head, tail], axis=-1).reshape(lead + (GMLP_HALF,))
    return jnp.moveaxis(out, -1, axis)


def _rope_tables(seq_len):
    t = jnp.arange(seq_len, dtype=jnp.int32)
    rows = seq_len // GRID_W
    row = (t // GRID_W - rows // 2).astype(F32)
    col = (t % GRID_W - GRID_W // 2).astype(F32)
    inv_freq = ROPE_THETA ** (-jnp.arange(ROPE_PAIRS_AXIS, dtype=F32) / ROPE_PAIRS_AXIS)
    ang = jnp.concatenate([row[:, None] * inv_freq, col[:, None] * inv_freq], axis=-1)
    cos, sin = jnp.cos(ang), jnp.sin(ang)
    reps = LANES // HEAD_DIM
    cos_t = jnp.tile(jnp.concatenate([cos, cos], axis=-1), (1, reps))
    sin_t = jnp.tile(jnp.concatenate([-sin, sin], axis=-1), (1, reps))
    return cos_t, sin_t


def kernel(x, c, ada_w, ada_b, norm1_g, norm2_g, attn_w_qkv, attn_q_norm_g, attn_k_norm_g,
           attn_w_o, gmlp_w_in, gmlp_b_in, gmlp_ln_g, gmlp_ln_b, gmlp_w_s, gmlp_b_s,
           gmlp_w_out, mlp_w_in, mlp_w_out):
    batch, seq, _ = x.shape
    mod = _mod_call(c, ada_w, ada_b)
    cos_t, sin_t = _rope_tables(seq)
    head_ids = jnp.arange(MXU_COLS) // HEAD_DIM
    gsum = (head_ids[:, None] == head_ids[None, :]).astype(BF16)
    q_scale = HEAD_DIM ** -0.5 * math.log2(math.e)

    for i in range(DEPTH):
        mod_l = mod[i].reshape(batch, 1, N_MOD * D_MODEL)
        j = i // 2
        n1 = norm1_g[i].reshape(1, D_MODEL)
        n2 = norm2_g[i].reshape(1, D_MODEL)
        w_in = mlp_w_in[i].astype(BF16)
        w_out = mlp_w_out[i].astype(BF16)
        if i % 2 == 0:
            w = attn_w_qkv[j]
            w_qkv = jnp.concatenate([_rope_order(w[:, :QK_WIDTH]), w[:, QK_WIDTH:]],
                                    axis=1).astype(BF16)
            qg = _rope_order(attn_q_norm_g[j]) * q_scale
            kg = _rope_order(attn_k_norm_g[j])
            qk_gain = jnp.concatenate([jnp.tile(qg, N_HEADS), jnp.tile(kg, N_KV_HEADS)])
            qt, k, vt = _qkv_call(x, mod_l, n1, w_qkv, gsum, qk_gain.reshape(1, QK_WIDTH),
                                  cos_t, sin_t)
            o = _attn_call(qt, k, vt)
            x = _mlp_call(x, mod_l, n2, w_in, w_out, attn=(o, attn_w_o[j].astype(BF16)))
        else:
            gw = gmlp_w_in[j]
            gw_in = jnp.concatenate([_pair_order(gw[:, :GMLP_HALF], 1),
                                     _pair_order(gw[:, GMLP_HALF:], 1)], axis=1).astype(BF16)
            gb = gmlp_b_in[j]
            b_in = jnp.concatenate([_pair_order(gb[:GMLP_HALF], 0),
                                    _pair_order(gb[GMLP_HALF:], 0)]).reshape(1, GMLP_FFN)
            ln_g = _pair_order(gmlp_ln_g[j], 0).reshape(1, GMLP_HALF)
            ln_b = _pair_order(gmlp_ln_b[j], 0).reshape(1, GMLP_HALF)
            bs_full = jnp.repeat(gmlp_b_s[j].T, GMLP_GROUP_WIDTH, axis=1)
            bs_table = _pair_order(bs_full, 1)
            gw_out = _pair_order(gmlp_w_out[j], 0).astype(BF16)
            x = _gmlp_call(x, mod_l, n1, gw_in, b_in, ln_g, ln_b, gmlp_w_s[j].astype(BF16),
                           bs_table, gw_out)
            x = _mlp_call(x, mod_l, n2, w_in, w_out)
    return x
```

```python
import math

import jax
import jax.numpy as jnp
from jax import lax
from jax.experimental import pallas as pl
from jax.experimental.pallas import tpu as pltpu

D_MODEL = 1024
DEPTH = 4
HEAD_DIM = 64
N_HEADS = 16
N_KV_HEADS = 4
KV_GROUP = N_HEADS // N_KV_HEADS
Q_WIDTH = N_HEADS * HEAD_DIM
KV_WIDTH = N_KV_HEADS * HEAD_DIM
QK_WIDTH = Q_WIDTH + KV_WIDTH
QKV_WIDTH = Q_WIDTH + 2 * KV_WIDTH
ROPE_THETA = 10000.0
ROPE_PAIRS_AXIS = HEAD_DIM // 4
GRID_W = 64
GMLP_FFN = 6 * D_MODEL
GMLP_HALF = GMLP_FFN // 2
GMLP_CHUNK = 128
GMLP_GROUPS = 16
GMLP_GROUP_WIDTH = GMLP_HALF // GMLP_GROUPS
MLP_HIDDEN = 4 * D_MODEL
N_MOD = 6
EPS = 1e-6

LANES = 128
MXU_COLS = 256
VMEM_LIMIT = 56 * 1024 * 1024

F32 = jnp.float32
BF16 = jnp.bfloat16


def _cparams(n_axes):
    return pltpu.CompilerParams(
        dimension_semantics=("arbitrary",) * n_axes,
        vmem_limit_bytes=VMEM_LIMIT,
    )


def _resident(block_shape, index_map):
    return pl.BlockSpec(block_shape, index_map, pipeline_mode=pl.Buffered(1))


def _split_bf16(a):
    hi = a.astype(BF16)
    lo = (a - hi.astype(F32)).astype(BF16)
    return hi, lo


def _modulated_rmsnorm(xv, gain, shift, scale):
    ms = jnp.mean(xv * xv, axis=-1, keepdims=True)
    y = xv * lax.rsqrt(ms + EPS) * gain
    return y * (1.0 + scale) + shift


MOD_TN = 1536


def _mod_kernel(c_ref, w_ref, b_ref, o_ref):
    cv = c_ref[...]
    cond = cv * jax.nn.sigmoid(cv)
    c_hi, c_lo = _split_bf16(cond)
    w_hi, w_lo = _split_bf16(w_ref[0])
    acc = jnp.dot(c_hi, w_hi, preferred_element_type=F32)
    acc += jnp.dot(c_lo, w_hi, preferred_element_type=F32)
    acc += jnp.dot(c_hi, w_lo, preferred_element_type=F32)
    o_ref[0] = acc + b_ref[0]


def _mod_call(c, ada_w, ada_b):
    batch = c.shape[0]
    n_out = N_MOD * D_MODEL
    return pl.pallas_call(
        _mod_kernel,
        grid=(DEPTH, n_out // MOD_TN),
        in_specs=[
            pl.BlockSpec((batch, D_MODEL), lambda l, j: (0, 0)),
            pl.BlockSpec((1, D_MODEL, MOD_TN), lambda l, j: (l, 0, j)),
            pl.BlockSpec((1, 1, MOD_TN), lambda l, j: (l, 0, j)),
        ],
        out_specs=pl.BlockSpec((1, batch, MOD_TN), lambda l, j: (l, 0, j)),
        out_shape=jax.ShapeDtypeStruct((DEPTH, batch, n_out), F32),
        compiler_params=_cparams(2),
        name="adaln_mod",
    )(c, ada_w, ada_b.reshape(DEPTH, 1, n_out))


QKV_TM = 1024
QKV_SUB = 256


def _qkv_kernel(x_ref, mod_ref, g_ref, w_ref, gsum_ref, qkg_ref, cos_ref, sin_ref,
                q_ref, k_ref, v_ref):
    mod = mod_ref[0]
    shift = mod[:, 0:D_MODEL]
    scale = mod[:, D_MODEL:2 * D_MODEL]
    gains = qkg_ref[...]
    lane = lax.broadcasted_iota(jnp.int32, (QKV_SUB, LANES), 1)
    even_lane = (lane & 1) == 0
    n_sub = x_ref.shape[1] // QKV_SUB

    def project(s):
        rows = slice(s * QKV_SUB, (s + 1) * QKV_SUB)
        h = _modulated_rmsnorm(x_ref[0, rows, :], g_ref[...], shift, scale).astype(BF16)
        return jnp.dot(h, w_ref[...], preferred_element_type=F32)

    def finish(s, acc):
        rows = slice(s * QKV_SUB, (s + 1) * QKV_SUB)
        v_ref[0, rows, :] = acc[:, QK_WIDTH:].astype(BF16)
        cosv = cos_ref[rows, :]
        sinv = sin_ref[rows, :]
        for j in range(QK_WIDTH // MXU_COLS):
            y = acc[:, j * MXU_COLS:(j + 1) * MXU_COLS]
            ss = jnp.dot((y * y).astype(BF16), gsum_ref[...], preferred_element_type=F32)
            yn = (y * lax.rsqrt(ss * (1.0 / HEAD_DIM) + EPS)
                  * gains[:, j * MXU_COLS:(j + 1) * MXU_COLS])
            for t in range(MXU_COLS // LANES):
                ys = yn[:, t * LANES:(t + 1) * LANES]
                partner = jnp.where(even_lane,
                                    pltpu.roll(ys, LANES - 1, 1),
                                    pltpu.roll(ys, 1, 1))
                r = (ys * cosv + partner * sinv).astype(BF16)
                col = j * MXU_COLS + t * LANES
                if col < Q_WIDTH:
                    q_ref[0, rows, col:col + LANES] = r
                else:
                    k_ref[0, rows, col - Q_WIDTH:col - Q_WIDTH + LANES] = r

    acc_next = project(0)
    for s in range(n_sub):
        acc = acc_next
        if s + 1 < n_sub:
            acc_next = project(s + 1)
        finish(s, acc)


def _qkv_call(x, mod_l, norm_g, w_qkv, gsum, qk_gain, cos_t, sin_t):
    batch, seq, _ = x.shape
    tm = QKV_TM
    return pl.pallas_call(
        _qkv_kernel,
        grid=(batch, seq // tm),
        in_specs=[
            pl.BlockSpec((1, tm, D_MODEL), lambda b, i: (b, i, 0)),
            pl.BlockSpec((1, 1, N_MOD * D_MODEL), lambda b, i: (b, 0, 0)),
            _resident((1, D_MODEL), lambda b, i: (0, 0)),
            _resident((D_MODEL, QKV_WIDTH), lambda b, i: (0, 0)),
            _resident((MXU_COLS, MXU_COLS), lambda b, i: (0, 0)),
            _resident((1, QK_WIDTH), lambda b, i: (0, 0)),
            pl.BlockSpec((tm, LANES), lambda b, i: (i, 0)),
            pl.BlockSpec((tm, LANES), lambda b, i: (i, 0)),
        ],
        out_specs=[
            pl.BlockSpec((1, tm, Q_WIDTH), lambda b, i: (b, i, 0)),
            pl.BlockSpec((1, tm, KV_WIDTH), lambda b, i: (b, i, 0)),
            pl.BlockSpec((1, tm, KV_WIDTH), lambda b, i: (b, i, 0)),
        ],
        out_shape=[
            jax.ShapeDtypeStruct((batch, seq, Q_WIDTH), BF16),
            jax.ShapeDtypeStruct((batch, seq, KV_WIDTH), BF16),
            jax.ShapeDtypeStruct((batch, seq, KV_WIDTH), BF16),
        ],
        compiler_params=_cparams(2),
        name="attn_qkv",
    )(x, mod_l, norm_g, w_qkv, gsum, qk_gain, cos_t, sin_t)


ATTN_TQ = 512
KV_PER_STEP = LANES // HEAD_DIM


def _attn_kernel(q_ref, k_ref, v_ref, o_ref):
    kk = k_ref[0]
    vv = v_ref[0]
    seq = kk.shape[0]
    tq = q_ref.shape[1]
    n_heads = KV_PER_STEP * KV_GROUP
    lane_kv = lax.broadcasted_iota(jnp.int32, (seq, LANES), 1)
    lane_q = lax.broadcasted_iota(jnp.int32, (tq, LANES), 1)
    v_aug = [jnp.where((lane_kv < HEAD_DIM) == (g == 0), vv, jnp.ones_like(vv))
             for g in range(KV_PER_STEP)]
    k_heads = [kk[:, g * HEAD_DIM:(g + 1) * HEAD_DIM] for g in range(KV_PER_STEP)]

    def scores(h):
        qh = q_ref[0, :, h * HEAD_DIM:(h + 1) * HEAD_DIM]
        s = lax.dot_general(qh, k_heads[h // KV_GROUP], (((1,), (1,)), ((), ())),
                            preferred_element_type=F32)
        return s, jnp.max(s, axis=-1, keepdims=True)

    nxt = scores(0)
    done = []
    for h in range(n_heads):
        s, m = nxt
        if h + 1 < n_heads:
            nxt = scores(h + 1)
        g = h // KV_GROUP
        p = jnp.exp2(s - m).astype(BF16)
        out = jnp.dot(p, v_aug[g], preferred_element_type=F32)
        if g == 0:
            o = out / out[:, HEAD_DIM:HEAD_DIM + 1]
        else:
            o = out / out[:, 0:1]
        done.append(o)
        if len(done) == 2:
            if g == 0:
                even, odd = done[0], pltpu.roll(done[1], HEAD_DIM, 1)
            else:
                even, odd = pltpu.roll(done[0], HEAD_DIM, 1), done[1]
            both = jnp.where(lane_q < HEAD_DIM, even, odd)
            o_ref[0, :, (h - 1) * HEAD_DIM:(h + 1) * HEAD_DIM] = both.astype(BF16)
            done = []


def _attn_call(q, k, v):
    batch, seq, _ = k.shape
    tq = ATTN_TQ
    qw = KV_PER_STEP * KV_GROUP * HEAD_DIM
    return pl.pallas_call(
        _attn_kernel,
        grid=(batch, N_KV_HEADS // KV_PER_STEP, seq // tq),
        in_specs=[
            pl.BlockSpec((1, tq, qw), lambda b, j, i: (b, i, j)),
            pl.BlockSpec((1, seq, LANES), lambda b, j, i: (b, 0, j)),
            pl.BlockSpec((1, seq, LANES), lambda b, j, i: (b, 0, j)),
        ],
        out_specs=pl.BlockSpec((1, tq, qw), lambda b, j, i: (b, i, j)),
        out_shape=jax.ShapeDtypeStruct((batch, seq, Q_WIDTH), BF16),
        compiler_params=_cparams(3),
        name="attn_core",
    )(q, k, v)


MLP_TM = 512
MLP_TH = 1024


def _mlp_body(xv, mod, g_ref, win_ref, wout_ref, out_ref, a_scr):
    shift = mod[:, 3 * D_MODEL:4 * D_MODEL]
    scale = mod[:, 4 * D_MODEL:5 * D_MODEL]
    gate = mod[:, 5 * D_MODEL:6 * D_MODEL]
    h = _modulated_rmsnorm(xv, g_ref[...], shift, scale).astype(BF16)
    for c in range(MLP_HIDDEN // MLP_TH):
        a = jnp.dot(h, win_ref[:, c * MLP_TH:(c + 1) * MLP_TH], preferred_element_type=F32)
        a = jnp.maximum(a, 0.0)
        a_scr[:, c * MLP_TH:(c + 1) * MLP_TH] = (a * a).astype(BF16)
    y = jnp.dot(a_scr[...], wout_ref[...], preferred_element_type=F32)
    out_ref[0] = xv + gate * y


def _mlp_kernel(x_ref, mod_ref, g_ref, win_ref, wout_ref, out_ref, a_scr):
    _mlp_body(x_ref[0], mod_ref[0], g_ref, win_ref, wout_ref, out_ref, a_scr)


def _proj_mlp_kernel(x_ref, o_ref, wo_ref, mod_ref, g_ref, win_ref, wout_ref, out_ref, a_scr):
    mod = mod_ref[0]
    gate1 = mod[:, 2 * D_MODEL:3 * D_MODEL]
    y = jnp.dot(o_ref[0], wo_ref[...], preferred_element_type=F32)
    _mlp_body(x_ref[0] + gate1 * y, mod, g_ref, win_ref, wout_ref, out_ref, a_scr)


def _mlp_call(x, mod_l, norm_g, w_in, w_out, attn=None):
    batch, seq, _ = x.shape
    tm = MLP_TM
    tile = pl.BlockSpec((1, tm, D_MODEL), lambda b, i: (b, i, 0))
    specs = [
        pl.BlockSpec((1, 1, N_MOD * D_MODEL), lambda b, i: (b, 0, 0)),
        _resident((1, D_MODEL), lambda b, i: (0, 0)),
        _resident((D_MODEL, MLP_HIDDEN), lambda b, i: (0, 0)),
        _resident((MLP_HIDDEN, D_MODEL), lambda b, i: (0, 0)),
    ]
    args = (mod_l, norm_g, w_in, w_out)
    if attn is None:
        body, pre_specs, pre_args = _mlp_kernel, [tile], (x,)
    else:
        body = _proj_mlp_kernel
        pre_specs = [tile, pl.BlockSpec((1, tm, Q_WIDTH), lambda b, i: (b, i, 0)),
                     _resident((Q_WIDTH, D_MODEL), lambda b, i: (0, 0))]
        pre_args = (x,) + tuple(attn)
    return pl.pallas_call(
        body,
        grid=(batch, seq // tm),
        in_specs=pre_specs + specs,
        out_specs=tile,
        out_shape=jax.ShapeDtypeStruct(x.shape, F32),
        scratch_shapes=[pltpu.VMEM((tm, MLP_HIDDEN), BF16)],
        compiler_params=_cparams(2),
        name="sq_relu_mlp" if attn is None else "proj_sq_relu_mlp",
    )(*pre_args, *args)


GMLP_TM = 512
GMLP_TN = 1024
PAIR_WIDTH = 2 * GMLP_GROUP_WIDTH


def _gelu(z):
    return 0.5 * z * (1.0 + lax.erf(z * (2.0 ** -0.5)))


def _gmlp_kernel(x_ref, mod_ref, g_ref, win_ref, bin_ref, lng_ref, lnb_ref, ws_ref, bs_ref,
                 wout_ref, out_ref, v_scr, u_scr, vn_scr, uv_scr):
    mod = mod_ref[0]
    shift = mod[:, 0:D_MODEL]
    scale = mod[:, D_MODEL:2 * D_MODEL]
    gate = mod[:, 2 * D_MODEL:3 * D_MODEL]
    xv = x_ref[0]
    tm = xv.shape[0]
    h = _modulated_rmsnorm(xv, g_ref[...], shift, scale).astype(BF16)
    n_col = GMLP_HALF // GMLP_TN

    s1 = jnp.zeros((tm, 1), F32)
    for c in range(n_col):
        lo = GMLP_HALF + c * GMLP_TN
        z = jnp.dot(h, win_ref[:, lo:lo + GMLP_TN], preferred_element_type=F32)
        zg = _gelu(z + bin_ref[:, lo:lo + GMLP_TN])
        v_scr[:, c * GMLP_TN:(c + 1) * GMLP_TN] = zg
        s1 += jnp.sum(zg, axis=-1, keepdims=True)
    for c in range(n_col):
        sl = slice(c * GMLP_TN, (c + 1) * GMLP_TN)
        z = jnp.dot(h, win_ref[:, sl], preferred_element_type=F32)
        u_scr[:, sl] = _gelu(z + bin_ref[:, sl])
    mu = s1 * (1.0 / GMLP_HALF)
    s2 = jnp.zeros((tm, 1), F32)
    for c in range(n_col):
        xc = v_scr[:, c * GMLP_TN:(c + 1) * GMLP_TN] - mu
        s2 += jnp.sum(xc * xc, axis=-1, keepdims=True)
    rstd = lax.rsqrt(s2 * (1.0 / GMLP_HALF) + EPS)
    for c in range(n_col):
        sl = slice(c * GMLP_TN, (c + 1) * GMLP_TN)
        vn = (v_scr[:, sl] - mu) * rstd * lng_ref[:, sl] + lnb_ref[:, sl]
        vn_scr[:, sl] = vn.astype(BF16)

    lane = lax.broadcasted_iota(jnp.int32, (GMLP_CHUNK, LANES), 1)
    low_lanes = lane < (GMLP_GROUP_WIDTH - LANES)
    for n in range(tm // GMLP_CHUNK):
        rows = slice(n * GMLP_CHUNK, (n + 1) * GMLP_CHUNK)
        for j in range(GMLP_GROUPS // 2):
            base = j * PAIR_WIDTH
            va = vn_scr[rows, base:base + LANES]
            vc = vn_scr[rows, base + LANES:base + 2 * LANES]
            vb = vn_scr[rows, base + 2 * LANES:base + 3 * LANES]
            ra = jnp.dot(ws_ref[2 * j], jnp.concatenate([va, vc], axis=1),
                         preferred_element_type=F32)
            rb = jnp.dot(ws_ref[2 * j + 1], jnp.concatenate([vb, vc], axis=1),
                         preferred_element_type=F32)
            rc = jnp.where(low_lanes, ra[:, LANES:], rb[:, LANES:])
            v_scr[rows, base:base + LANES] = ra[:, :LANES] + bs_ref[:, base:base + LANES]
            v_scr[rows, base + LANES:base + 2 * LANES] = (
                rc + bs_ref[:, base + LANES:base + 2 * LANES])
            v_scr[rows, base + 2 * LANES:base + 3 * LANES] = (
                rb[:, :LANES] + bs_ref[:, base + 2 * LANES:base + 3 * LANES])

    for c in range(n_col):
        sl = slice(c * GMLP_TN, (c + 1) * GMLP_TN)
        uv_scr[:, sl] = (u_scr[:, sl] * v_scr[:, sl]).astype(BF16)
    y = jnp.dot(uv_scr[...], wout_ref[...], preferred_element_type=F32)
    out_ref[0] = xv + gate * y


def _gmlp_call(x, mod_l, norm_g, w_in, b_in, ln_g, ln_b, w_s, bs_table, w_out):
    batch, seq, _ = x.shape
    tm = GMLP_TM
    return pl.pallas_call(
        _gmlp_kernel,
        grid=(batch, seq // tm),
        in_specs=[
            pl.BlockSpec((1, tm, D_MODEL), lambda b, i: (b, i, 0)),
            pl.BlockSpec((1, 1, N_MOD * D_MODEL), lambda b, i: (b, 0, 0)),
            _resident((1, D_MODEL), lambda b, i: (0, 0)),
            _resident((D_MODEL, GMLP_FFN), lambda b, i: (0, 0)),
            _resident((1, GMLP_FFN), lambda b, i: (0, 0)),
            _resident((1, GMLP_HALF), lambda b, i: (0, 0)),
            _resident((1, GMLP_HALF), lambda b, i: (0, 0)),
            _resident((GMLP_GROUPS, GMLP_CHUNK, GMLP_CHUNK), lambda b, i: (0, 0, 0)),
            _resident((GMLP_CHUNK, GMLP_HALF), lambda b, i: (0, 0)),
            _resident((GMLP_HALF, D_MODEL), lambda b, i: (0, 0)),
        ],
        out_specs=pl.BlockSpec((1, tm, D_MODEL), lambda b, i: (b, i, 0)),
        out_shape=jax.ShapeDtypeStruct(x.shape, F32),
        scratch_shapes=[
            pltpu.VMEM((tm, GMLP_HALF), F32),
            pltpu.VMEM((tm, GMLP_HALF), F32),
            pltpu.VMEM((tm, GMLP_HALF), BF16),
            pltpu.VMEM((tm, GMLP_HALF), BF16),
        ],
        compiler_params=_cparams(2),
        name="gmlp_mixer",
    )(x, mod_l, norm_g, w_in, b_in, ln_g, ln_b, w_s, bs_table, w_out)


def _rope_tables(seq_len):
    t = jnp.arange(seq_len, dtype=jnp.int32)
    rows = seq_len // GRID_W
    row = (t // GRID_W - rows // 2).astype(F32)
    col = (t % GRID_W - GRID_W // 2).astype(F32)
    inv_freq = ROPE_THETA ** (-jnp.arange(ROPE_PAIRS_AXIS, dtype=F32) / ROPE_PAIRS_AXIS)
    ang = jnp.concatenate([row[:, None] * inv_freq, col[:, None] * inv_freq], axis=-1)
    reps = LANES // HEAD_DIM
    cos_t = jnp.tile(jnp.repeat(jnp.cos(ang), 2, axis=-1), (1, reps))
    sin_t = jnp.tile(jnp.repeat(jnp.sin(ang), 2, axis=-1), (1, reps))
    sign = jnp.tile(jnp.asarray([-1.0, 1.0], F32), LANES // 2)
    return cos_t, sin_t * sign


def kernel(x, c, ada_w, ada_b, norm1_g, norm2_g, attn_w_qkv, attn_q_norm_g, attn_k_norm_g,
           attn_w_o, gmlp_w_in, gmlp_b_in, gmlp_ln_g, gmlp_ln_b, gmlp_w_s, gmlp_b_s,
           gmlp_w_out, mlp_w_in, mlp_w_out):
    batch, seq, _ = x.shape
    mod = _mod_call(c, ada_w, ada_b)
    cos_t, sin_t = _rope_tables(seq)
    head_ids = jnp.arange(MXU_COLS) // HEAD_DIM
    gsum = (head_ids[:, None] == head_ids[None, :]).astype(BF16)
    q_scale = HEAD_DIM ** -0.5 * math.log2(math.e)

    mod = mod.reshape(DEPTH, batch, 1, N_MOD * D_MODEL)
    mlp_w_in, mlp_w_out = mlp_w_in.astype(BF16), mlp_w_out.astype(BF16)
    attn_w_qkv, attn_w_o = attn_w_qkv.astype(BF16), attn_w_o.astype(BF16)
    gmlp_w_in, gmlp_w_out = gmlp_w_in.astype(BF16), gmlp_w_out.astype(BF16)
    gmlp_w_s = gmlp_w_s.astype(BF16)

    for i in range(DEPTH):
        j = i // 2
        n1 = norm1_g[i].reshape(1, D_MODEL)
        n2 = norm2_g[i].reshape(1, D_MODEL)
        if i % 2 == 0:
            qk_gain = jnp.concatenate([jnp.tile(attn_q_norm_g[j] * q_scale, N_HEADS),
                                       jnp.tile(attn_k_norm_g[j], N_KV_HEADS)])
            q, k, v = _qkv_call(x, mod[i], n1, attn_w_qkv[j], gsum,
                                qk_gain.reshape(1, QK_WIDTH), cos_t, sin_t)
            o = _attn_call(q, k, v)
            x = _mlp_call(x, mod[i], n2, mlp_w_in[i], mlp_w_out[i], attn=(o, attn_w_o[j]))
        else:
            bs_table = jnp.repeat(gmlp_b_s[j].T, GMLP_GROUP_WIDTH, axis=1)
            x = _gmlp_call(x, mod[i], n1, gmlp_w_in[j], gmlp_b_in[j].reshape(1, GMLP_FFN),
                           gmlp_ln_g[j].reshape(1, GMLP_HALF), gmlp_ln_b[j].reshape(1, GMLP_HALF),
                           gmlp_w_s[j], bs_table, gmlp_w_out[j])
            x = _mlp_call(x, mod[i], n2, mlp_w_in[i], mlp_w_out[i])
    return x
```

```python
import math

import jax
import jax.numpy as jnp
from jax import lax
from jax.experimental import pallas as pl
from jax.experimental.pallas import tpu as pltpu

D_MODEL = 1024
DEPTH = 4
HEAD_DIM = 64
N_HEADS = 16
N_KV_HEADS = 4
KV_GROUP = N_HEADS // N_KV_HEADS
Q_WIDTH = N_HEADS * HEAD_DIM
KV_WIDTH = N_KV_HEADS * HEAD_DIM
QK_WIDTH = Q_WIDTH + KV_WIDTH
QKV_WIDTH = Q_WIDTH + 2 * KV_WIDTH
ROPE_THETA = 10000.0
ROPE_PAIRS_AXIS = HEAD_DIM // 4
GRID_W = 64
GMLP_FFN = 6 * D_MODEL
GMLP_HALF = GMLP_FFN // 2
GMLP_CHUNK = 128
GMLP_GROUPS = 16
GMLP_GROUP_WIDTH = GMLP_HALF // GMLP_GROUPS
MLP_HIDDEN = 4 * D_MODEL
N_MOD = 6
EPS = 1e-6

LANES = 128
MXU_COLS = 256
VMEM_LIMIT = 56 * 1024 * 1024

F32 = jnp.float32
BF16 = jnp.bfloat16


def _cparams(n_axes):
    return pltpu.CompilerParams(
        dimension_semantics=("arbitrary",) * n_axes,
        vmem_limit_bytes=VMEM_LIMIT,
    )


def _resident(block_shape, index_map):
    return pl.BlockSpec(block_shape, index_map, pipeline_mode=pl.Buffered(1))


def _layer_slab(stacked, layer):
    tail = stacked.shape[1:]
    return _resident((None,) + tail, lambda b, i: (layer,) + (0,) * len(tail))


def _mod_spec(layer):
    return pl.BlockSpec((None, 1, 1, N_MOD * D_MODEL), lambda b, i: (layer, b, 0, 0))


def _split_bf16(a):
    hi = a.astype(BF16)
    lo = (a - hi.astype(F32)).astype(BF16)
    return hi, lo


def _modulated_rmsnorm(xv, gain, shift, scale):
    ms = jnp.mean(xv * xv, axis=-1, keepdims=True)
    y = xv * lax.rsqrt(ms + EPS) * gain
    return y * (1.0 + scale) + shift


MOD_TN = 1536


def _mod_kernel(c_ref, w_ref, b_ref, o_ref):
    cv = c_ref[...]
    cond = cv * jax.nn.sigmoid(cv)
    c_hi, c_lo = _split_bf16(cond)
    w_hi, w_lo = _split_bf16(w_ref[0])
    acc = jnp.dot(c_hi, w_hi, preferred_element_type=F32)
    acc += jnp.dot(c_lo, w_hi, preferred_element_type=F32)
    acc += jnp.dot(c_hi, w_lo, preferred_element_type=F32)
    o_ref[0] = acc + b_ref[0]


def _mod_call(c, ada_w, ada_b):
    batch = c.shape[0]
    n_out = N_MOD * D_MODEL
    return pl.pallas_call(
        _mod_kernel,
        grid=(DEPTH, n_out // MOD_TN),
        in_specs=[
            pl.BlockSpec((batch, D_MODEL), lambda l, j: (0, 0)),
            pl.BlockSpec((1, D_MODEL, MOD_TN), lambda l, j: (l, 0, j)),
            pl.BlockSpec((1, 1, MOD_TN), lambda l, j: (l, 0, j)),
        ],
        out_specs=pl.BlockSpec((1, batch, MOD_TN), lambda l, j: (l, 0, j)),
        out_shape=jax.ShapeDtypeStruct((DEPTH, batch, n_out), F32),
        compiler_params=_cparams(2),
        name="adaln_mod",
    )(c, ada_w, ada_b.reshape(DEPTH, 1, n_out))


QKV_TM = 1024
QKV_SUB = 256


def _qkv_kernel(x_ref, mod_ref, g_ref, w_ref, gsum_ref, qkg_ref, cos_ref, sin_ref,
                qt_ref, k_ref, vt_ref):
    mod = mod_ref[0]
    shift = mod[:, 0:D_MODEL]
    scale = mod[:, D_MODEL:2 * D_MODEL]
    gains = qkg_ref[...]
    lane = lax.broadcasted_iota(jnp.int32, (QKV_SUB, LANES), 1)
    even_lane = (lane & 1) == 0
    n_sub = x_ref.shape[1] // QKV_SUB

    def project(s):
        rows = slice(s * QKV_SUB, (s + 1) * QKV_SUB)
        h = _modulated_rmsnorm(x_ref[0, rows, :], g_ref[...], shift, scale).astype(BF16)
        return jnp.dot(h, w_ref[...], preferred_element_type=F32)

    def finish(s, acc):
        rows = slice(s * QKV_SUB, (s + 1) * QKV_SUB)
        vt_ref[0, :, rows] = acc[:, QK_WIDTH:].T.astype(BF16)
        cosv = cos_ref[rows, :]
        sinv = sin_ref[rows, :]
        for j in range(QK_WIDTH // MXU_COLS):
            y = acc[:, j * MXU_COLS:(j + 1) * MXU_COLS]
            ss = jnp.dot((y * y).astype(BF16), gsum_ref[...], preferred_element_type=F32)
            yn = (y * lax.rsqrt(ss * (1.0 / HEAD_DIM) + EPS)
                  * gains[:, j * MXU_COLS:(j + 1) * MXU_COLS])
            for t in range(MXU_COLS // LANES):
                ys = yn[:, t * LANES:(t + 1) * LANES]
                partner = jnp.where(even_lane,
                                    pltpu.roll(ys, LANES - 1, 1),
                                    pltpu.roll(ys, 1, 1))
                r = ys * cosv + partner * sinv
                col = j * MXU_COLS + t * LANES
                if col < Q_WIDTH:
                    qt_ref[0, col:col + LANES, rows] = r.T.astype(BF16)
                else:
                    k_ref[0, rows, col - Q_WIDTH:col - Q_WIDTH + LANES] = r.astype(BF16)

    acc_next = project(0)
    for s in range(n_sub):
        acc = acc_next
        if s + 1 < n_sub:
            acc_next = project(s + 1)
        finish(s, acc)


def _qkv_call(x, mod, layer, norm_g, w_qkv, attn_layer, gsum, qk_gain, cos_t, sin_t):
    batch, seq, _ = x.shape
    tm = QKV_TM
    return pl.pallas_call(
        _qkv_kernel,
        grid=(batch, seq // tm),
        in_specs=[
            pl.BlockSpec((1, tm, D_MODEL), lambda b, i: (b, i, 0)),
            _mod_spec(layer),
            _layer_slab(norm_g, layer),
            _layer_slab(w_qkv, attn_layer),
            _resident((MXU_COLS, MXU_COLS), lambda b, i: (0, 0)),
            _resident((1, QK_WIDTH), lambda b, i: (0, 0)),
            pl.BlockSpec((tm, LANES), lambda b, i: (i, 0)),
            pl.BlockSpec((tm, LANES), lambda b, i: (i, 0)),
        ],
        out_specs=[
            pl.BlockSpec((1, Q_WIDTH, tm), lambda b, i: (b, 0, i)),
            pl.BlockSpec((1, tm, KV_WIDTH), lambda b, i: (b, i, 0)),
            pl.BlockSpec((1, KV_WIDTH, tm), lambda b, i: (b, 0, i)),
        ],
        out_shape=[
            jax.ShapeDtypeStruct((batch, Q_WIDTH, seq), BF16),
            jax.ShapeDtypeStruct((batch, seq, KV_WIDTH), BF16),
            jax.ShapeDtypeStruct((batch, KV_WIDTH, seq), BF16),
        ],
        compiler_params=_cparams(2),
        name="attn_qkv",
    )(x, mod, norm_g, w_qkv, gsum, qk_gain, cos_t, sin_t)


ATTN_TQ = 512
ATTN_KC = 256
ATTN_SLOTS = 3
ONES_ROWS = 16
KV_PER_STEP = LANES // HEAD_DIM


def _attn_kernel(qt_ref, k_ref, vt_ref, o_ref, *st_scr):
    n_slots = len(st_scr)
    ahead = n_slots - 1
    seq = k_ref.shape[1]
    tq = qt_ref.shape[2]
    n_heads = KV_PER_STEP * KV_GROUP
    n_kc = seq // ATTN_KC
    zeros = jnp.zeros((HEAD_DIM, tq), BF16)
    ones = jnp.ones((ONES_ROWS, ATTN_KC), BF16)

    def q_operand(h):
        qt = qt_ref[0, h * HEAD_DIM:(h + 1) * HEAD_DIM, :]
        return jnp.concatenate([qt, zeros] if h < KV_GROUP else [zeros, qt], axis=0)

    def score_chunk(h, qt_pad, c):
        rows = slice(c * ATTN_KC, (c + 1) * ATTN_KC)
        st = jnp.dot(k_ref[0, rows, :], qt_pad, preferred_element_type=F32)
        st_scr[h % n_slots][rows, :] = st
        return jnp.max(st, axis=0, keepdims=True)

    def reduce_max(parts):
        while len(parts) > 1:
            parts = [jnp.maximum(a, b) for a, b in zip(parts[0::2], parts[1::2])]
        return parts[0]

    maxes = {}
    for h in range(min(ahead, n_heads)):
        qt_pad = q_operand(h)
        maxes[h] = reduce_max([score_chunk(h, qt_pad, c) for c in range(n_kc)])
    done = []
    for h in range(n_heads):
        m = maxes.pop(h)
        g = h // KV_GROUP
        nxt = h + ahead
        if nxt < n_heads:
            qt_pad = q_operand(nxt)
        parts = []
        acc = jnp.zeros((HEAD_DIM + ONES_ROWS, tq), F32)
        for c in range(n_kc):
            rows = slice(c * ATTN_KC, (c + 1) * ATTN_KC)
            if nxt < n_heads:
                parts.append(score_chunk(nxt, qt_pad, c))
            pt = jnp.exp2(st_scr[h % n_slots][rows, :] - m).astype(BF16)
            vt_aug = jnp.concatenate([vt_ref[0, g * HEAD_DIM:(g + 1) * HEAD_DIM, rows], ones],
                                     axis=0)
            acc = acc + jnp.dot(vt_aug, pt, preferred_element_type=F32)
        if parts:
            maxes[nxt] = reduce_max(parts)
        done.append(acc[:HEAD_DIM] / acc[HEAD_DIM:HEAD_DIM + 1])
        if len(done) == 2:
            both = jnp.concatenate(done, axis=0)
            o_ref[0, :, (h - 1) * HEAD_DIM:(h + 1) * HEAD_DIM] = both.T.astype(BF16)
            done = []


def _attn_call(qt, k, vt):
    batch, seq, _ = k.shape
    tq = ATTN_TQ
    qw = KV_PER_STEP * KV_GROUP * HEAD_DIM
    return pl.pallas_call(
        _attn_kernel,
        grid=(batch, N_KV_HEADS // KV_PER_STEP, seq // tq),
        in_specs=[
            pl.BlockSpec((1, qw, tq), lambda b, j, i: (b, j, i)),
            pl.BlockSpec((1, seq, LANES), lambda b, j, i: (b, 0, j)),
            pl.BlockSpec((1, LANES, seq), lambda b, j, i: (b, j, 0)),
        ],
        out_specs=pl.BlockSpec((1, tq, qw), lambda b, j, i: (b, i, j)),
        out_shape=jax.ShapeDtypeStruct((batch, seq, Q_WIDTH), BF16),
        scratch_shapes=[pltpu.VMEM((seq, tq), F32)] * ATTN_SLOTS,
        compiler_params=_cparams(3),
        name="attn_core",
    )(qt, k, vt)


MLP_TM = 512
MLP_TH = 1024


def _mlp_body(xv, mod, g_ref, win_ref, wout_ref, out_ref, a_scr):
    shift = mod[:, 3 * D_MODEL:4 * D_MODEL]
    scale = mod[:, 4 * D_MODEL:5 * D_MODEL]
    gate = mod[:, 5 * D_MODEL:6 * D_MODEL]
    h = _modulated_rmsnorm(xv, g_ref[...], shift, scale).astype(BF16)
    for c in range(MLP_HIDDEN // MLP_TH):
        a = jnp.dot(h, win_ref[:, c * MLP_TH:(c + 1) * MLP_TH], preferred_element_type=F32)
        a = jnp.maximum(a, 0.0)
        a_scr[:, c * MLP_TH:(c + 1) * MLP_TH] = (a * a).astype(BF16)
    y = jnp.dot(a_scr[...], wout_ref[...], preferred_element_type=F32)
    out_ref[0] = xv + gate * y


def _mlp_kernel(x_ref, mod_ref, g_ref, win_ref, wout_ref, out_ref, a_scr):
    _mlp_body(x_ref[0], mod_ref[0], g_ref, win_ref, wout_ref, out_ref, a_scr)


def _proj_mlp_kernel(x_ref, o_ref, wo_ref, mod_ref, g_ref, win_ref, wout_ref, out_ref, a_scr):
    mod = mod_ref[0]
    gate1 = mod[:, 2 * D_MODEL:3 * D_MODEL]
    y = jnp.dot(o_ref[0], wo_ref[...], preferred_element_type=F32)
    _mlp_body(x_ref[0] + gate1 * y, mod, g_ref, win_ref, wout_ref, out_ref, a_scr)


def _mlp_call(x, mod, layer, norm_g, w_in, w_out, attn=None):
    batch, seq, _ = x.shape
    tm = MLP_TM
    tile = pl.BlockSpec((1, tm, D_MODEL), lambda b, i: (b, i, 0))
    specs = [_mod_spec(layer), _layer_slab(norm_g, layer), _layer_slab(w_in, layer),
             _layer_slab(w_out, layer)]
    args = (mod, norm_g, w_in, w_out)
    if attn is None:
        body, pre_specs, pre_args = _mlp_kernel, [tile], (x,)
    else:
        o, w_o, attn_layer = attn
        body = _proj_mlp_kernel
        pre_specs = [tile, pl.BlockSpec((1, tm, Q_WIDTH), lambda b, i: (b, i, 0)),
                     _layer_slab(w_o, attn_layer)]
        pre_args = (x, o, w_o)
    return pl.pallas_call(
        body,
        grid=(batch, seq // tm),
        in_specs=pre_specs + specs,
        out_specs=tile,
        out_shape=jax.ShapeDtypeStruct(x.shape, F32),
        scratch_shapes=[pltpu.VMEM((tm, MLP_HIDDEN), BF16)],
        compiler_params=_cparams(2),
        name="sq_relu_mlp" if attn is None else "proj_sq_relu_mlp",
    )(*pre_args, *args)


GMLP_TM = 512
GMLP_TN = 1024
PAIR_WIDTH = 2 * GMLP_GROUP_WIDTH


def _gelu(z):
    return 0.5 * z * (1.0 + lax.erf(z * (2.0 ** -0.5)))


def _gmlp_kernel(x_ref, mod_ref, g_ref, win_ref, bin_ref, lng_ref, lnb_ref, ws_ref, bs_ref,
                 wout_ref, out_ref, v_scr, u_scr, vn_scr, uv_scr):
    mod = mod_ref[0]
    shift = mod[:, 0:D_MODEL]
    scale = mod[:, D_MODEL:2 * D_MODEL]
    gate = mod[:, 2 * D_MODEL:3 * D_MODEL]
    xv = x_ref[0]
    tm = xv.shape[0]
    h = _modulated_rmsnorm(xv, g_ref[...], shift, scale).astype(BF16)
    n_col = GMLP_HALF // GMLP_TN

    s1 = jnp.zeros((tm, 1), F32)
    for c in range(n_col):
        lo = GMLP_HALF + c * GMLP_TN
        z = jnp.dot(h, win_ref[:, lo:lo + GMLP_TN], preferred_element_type=F32)
        zg = _gelu(z + bin_ref[:, lo:lo + GMLP_TN])
        v_scr[:, c * GMLP_TN:(c + 1) * GMLP_TN] = zg
        s1 += jnp.sum(zg, axis=-1, keepdims=True)
    for c in range(n_col):
        sl = slice(c * GMLP_TN, (c + 1) * GMLP_TN)
        z = jnp.dot(h, win_ref[:, sl], preferred_element_type=F32)
        u_scr[:, sl] = _gelu(z + bin_ref[:, sl])
    mu = s1 * (1.0 / GMLP_HALF)
    s2 = jnp.zeros((tm, 1), F32)
    for c in range(n_col):
        xc = v_scr[:, c * GMLP_TN:(c + 1) * GMLP_TN] - mu
        s2 += jnp.sum(xc * xc, axis=-1, keepdims=True)
    rstd = lax.rsqrt(s2 * (1.0 / GMLP_HALF) + EPS)
    for c in range(n_col):
        sl = slice(c * GMLP_TN, (c + 1) * GMLP_TN)
        vn = (v_scr[:, sl] - mu) * rstd * lng_ref[:, sl] + lnb_ref[:, sl]
        vn_scr[:, sl] = vn.astype(BF16)

    lane = lax.broadcasted_iota(jnp.int32, (GMLP_CHUNK, LANES), 1)
    low_lanes = lane < (GMLP_GROUP_WIDTH - LANES)
    for n in range(tm // GMLP_CHUNK):
        rows = slice(n * GMLP_CHUNK, (n + 1) * GMLP_CHUNK)
        for j in range(GMLP_GROUPS // 2):
            base = j * PAIR_WIDTH
            va = vn_scr[rows, base:base + LANES]
            vc = vn_scr[rows, base + LANES:base + 2 * LANES]
            vb = vn_scr[rows, base + 2 * LANES:base + 3 * LANES]
            ra = jnp.dot(ws_ref[2 * j], jnp.concatenate([va, vc], axis=1),
                         preferred_element_type=F32)
            rb = jnp.dot(ws_ref[2 * j + 1], jnp.concatenate([vb, vc], axis=1),
                         preferred_element_type=F32)
            rc = jnp.where(low_lanes, ra[:, LANES:], rb[:, LANES:])
            v_scr[rows, base:base + LANES] = ra[:, :LANES] + bs_ref[:, base:base + LANES]
            v_scr[rows, base + LANES:base + 2 * LANES] = (
                rc + bs_ref[:, base + LANES:base + 2 * LANES])
            v_scr[rows, base + 2 * LANES:base + 3 * LANES] = (
                rb[:, :LANES] + bs_ref[:, base + 2 * LANES:base + 3 * LANES])

    for c in range(n_col):
        sl = slice(c * GMLP_TN, (c + 1) * GMLP_TN)
        uv_scr[:, sl] = (u_scr[:, sl] * v_scr[:, sl]).astype(BF16)
    y = jnp.dot(uv_scr[...], wout_ref[...], preferred_element_type=F32)
    out_ref[0] = xv + gate * y


def _gmlp_call(x, mod, layer, norm_g, gmlp_layer, w_in, b_in, ln_g, ln_b, w_s, bs_table, w_out):
    batch, seq, _ = x.shape
    tm = GMLP_TM
    gmlp_params = (w_in, b_in, ln_g, ln_b, w_s, bs_table, w_out)
    return pl.pallas_call(
        _gmlp_kernel,
        grid=(batch, seq // tm),
        in_specs=[
            pl.BlockSpec((1, tm, D_MODEL), lambda b, i: (b, i, 0)),
            _mod_spec(layer),
            _layer_slab(norm_g, layer),
        ] + [_layer_slab(p, gmlp_layer) for p in gmlp_params],
        out_specs=pl.BlockSpec((1, tm, D_MODEL), lambda b, i: (b, i, 0)),
        out_shape=jax.ShapeDtypeStruct(x.shape, F32),
        scratch_shapes=[
            pltpu.VMEM((tm, GMLP_HALF), F32),
            pltpu.VMEM((tm, GMLP_HALF), F32),
            pltpu.VMEM((tm, GMLP_HALF), BF16),
            pltpu.VMEM((tm, GMLP_HALF), BF16),
        ],
        compiler_params=_cparams(2),
        name="gmlp_mixer",
    )(x, mod, norm_g, *gmlp_params)


def _rope_tables(seq_len):
    t = jnp.arange(seq_len, dtype=jnp.int32)
    rows = seq_len // GRID_W
    row = (t // GRID_W - rows // 2).astype(F32)
    col = (t % GRID_W - GRID_W // 2).astype(F32)
    inv_freq = ROPE_THETA ** (-jnp.arange(ROPE_PAIRS_AXIS, dtype=F32) / ROPE_PAIRS_AXIS)
    ang = jnp.concatenate([row[:, None] * inv_freq, col[:, None] * inv_freq], axis=-1)
    reps = LANES // HEAD_DIM
    cos_t = jnp.tile(jnp.repeat(jnp.cos(ang), 2, axis=-1), (1, reps))
    sin_t = jnp.tile(jnp.repeat(jnp.sin(ang), 2, axis=-1), (1, reps))
    sign = jnp.tile(jnp.asarray([-1.0, 1.0], F32), LANES // 2)
    return cos_t, sin_t * sign


def kernel(x, c, ada_w, ada_b, norm1_g, norm2_g, attn_w_qkv, attn_q_norm_g, attn_k_norm_g,
           attn_w_o, gmlp_w_in, gmlp_b_in, gmlp_ln_g, gmlp_ln_b, gmlp_w_s, gmlp_b_s,
           gmlp_w_out, mlp_w_in, mlp_w_out):
    batch, seq, _ = x.shape
    mod = _mod_call(c, ada_w, ada_b)
    cos_t, sin_t = _rope_tables(seq)
    head_ids = jnp.arange(MXU_COLS) // HEAD_DIM
    gsum = (head_ids[:, None] == head_ids[None, :]).astype(BF16)
    q_scale = HEAD_DIM ** -0.5 * math.log2(math.e)

    mod = mod.reshape(DEPTH, batch, 1, N_MOD * D_MODEL)
    mlp_w_in, mlp_w_out = mlp_w_in.astype(BF16), mlp_w_out.astype(BF16)
    attn_w_qkv, attn_w_o = attn_w_qkv.astype(BF16), attn_w_o.astype(BF16)
    gmlp_w_in, gmlp_w_out = gmlp_w_in.astype(BF16), gmlp_w_out.astype(BF16)
    gmlp_w_s = gmlp_w_s.astype(BF16)
    n1 = norm1_g[:, None, :]
    n2 = norm2_g[:, None, :]
    b_in = gmlp_b_in[:, None, :]
    ln_g = gmlp_ln_g[:, None, :]
    ln_b = gmlp_ln_b[:, None, :]
    bs_table = jnp.repeat(jnp.swapaxes(gmlp_b_s, 1, 2), GMLP_GROUP_WIDTH, axis=2)

    for i in range(DEPTH):
        j = i // 2
        if i % 2 == 0:
            qk_gain = jnp.concatenate([jnp.tile(attn_q_norm_g[j] * q_scale, N_HEADS),
                                       jnp.tile(attn_k_norm_g[j], N_KV_HEADS)])
            qt, k, vt = _qkv_call(x, mod, i, n1, attn_w_qkv, j, gsum,
                                  qk_gain.reshape(1, QK_WIDTH), cos_t, sin_t)
            o = _attn_call(qt, k, vt)
            x = _mlp_call(x, mod, i, n2, mlp_w_in, mlp_w_out, attn=(o, attn_w_o, j))
        else:
            x = _gmlp_call(x, mod, i, n1, j, gmlp_w_in, b_in, ln_g, ln_b, gmlp_w_s, bs_table,
                           gmlp_w_out)
            x = _mlp_call(x, mod, i, n2, mlp_w_in, mlp_w_out)
    return x
```

```python
import math

import jax
import jax.numpy as jnp
from jax import lax
from jax.experimental import pallas as pl
from jax.experimental.pallas import tpu as pltpu

D_MODEL = 1024
DEPTH = 4
HEAD_DIM = 64
N_HEADS = 16
N_KV_HEADS = 4
KV_GROUP = N_HEADS // N_KV_HEADS
Q_WIDTH = N_HEADS * HEAD_DIM
KV_WIDTH = N_KV_HEADS * HEAD_DIM
QK_WIDTH = Q_WIDTH + KV_WIDTH
QKV_WIDTH = Q_WIDTH + 2 * KV_WIDTH
ROPE_THETA = 10000.0
ROPE_PAIRS_AXIS = HEAD_DIM // 4
GRID_W = 64
GMLP_FFN = 6 * D_MODEL
GMLP_HALF = GMLP_FFN // 2
GMLP_CHUNK = 128
GMLP_GROUPS = 16
GMLP_GROUP_WIDTH = GMLP_HALF // GMLP_GROUPS
MLP_HIDDEN = 4 * D_MODEL
N_MOD = 6
EPS = 1e-6

LANES = 128
MXU_COLS = 256
VMEM_LIMIT = 56 * 1024 * 1024

F32 = jnp.float32
BF16 = jnp.bfloat16


def _cparams(n_axes):
    return pltpu.CompilerParams(
        dimension_semantics=("arbitrary",) * n_axes,
        vmem_limit_bytes=VMEM_LIMIT,
    )


def _resident(block_shape, index_map):
    return pl.BlockSpec(block_shape, index_map, pipeline_mode=pl.Buffered(1))


def _layer_slab(stacked, layer):
    tail = stacked.shape[1:]
    return _resident((None,) + tail, lambda b, i: (layer,) + (0,) * len(tail))


def _mod_spec(layer):
    return pl.BlockSpec((None, 1, 1, N_MOD * D_MODEL), lambda b, i: (layer, b, 0, 0))


def _split_bf16(a):
    hi = a.astype(BF16)
    lo = (a - hi.astype(F32)).astype(BF16)
    return hi, lo


def _modulated_rmsnorm(xv, gain, shift, scale):
    ms = jnp.mean(xv * xv, axis=-1, keepdims=True)
    y = xv * lax.rsqrt(ms + EPS) * gain
    return y * (1.0 + scale) + shift


MOD_TN = 1536


def _mod_kernel(c_ref, w_ref, b_ref, o_ref):
    cv = c_ref[...]
    cond = cv * jax.nn.sigmoid(cv)
    c_hi, c_lo = _split_bf16(cond)
    w_hi, w_lo = _split_bf16(w_ref[0])
    acc = jnp.dot(c_hi, w_hi, preferred_element_type=F32)
    acc += jnp.dot(c_lo, w_hi, preferred_element_type=F32)
    acc += jnp.dot(c_hi, w_lo, preferred_element_type=F32)
    o_ref[0] = acc + b_ref[0]


def _mod_call(c, ada_w, ada_b):
    batch = c.shape[0]
    n_out = N_MOD * D_MODEL
    return pl.pallas_call(
        _mod_kernel,
        grid=(DEPTH, n_out // MOD_TN),
        in_specs=[
            pl.BlockSpec((batch, D_MODEL), lambda l, j: (0, 0)),
            pl.BlockSpec((1, D_MODEL, MOD_TN), lambda l, j: (l, 0, j)),
            pl.BlockSpec((1, 1, MOD_TN), lambda l, j: (l, 0, j)),
        ],
        out_specs=pl.BlockSpec((1, batch, MOD_TN), lambda l, j: (l, 0, j)),
        out_shape=jax.ShapeDtypeStruct((DEPTH, batch, n_out), F32),
        compiler_params=_cparams(2),
        name="adaln_mod",
    )(c, ada_w, ada_b.reshape(DEPTH, 1, n_out))


QKV_TM = 1024
QKV_SUB = 256


def _qkv_kernel(x_ref, mod_ref, g_ref, w_ref, gsum_ref, qkg_ref, cos_ref, sin_ref,
                qt_ref, k_ref, vt_ref):
    mod = mod_ref[0]
    shift = mod[:, 0:D_MODEL]
    scale = mod[:, D_MODEL:2 * D_MODEL]
    gains = qkg_ref[...]
    lane = lax.broadcasted_iota(jnp.int32, (QKV_SUB, LANES), 1)
    even_lane = (lane & 1) == 0
    n_sub = x_ref.shape[1] // QKV_SUB

    def project(s):
        rows = slice(s * QKV_SUB, (s + 1) * QKV_SUB)
        h = _modulated_rmsnorm(x_ref[0, rows, :], g_ref[...], shift, scale).astype(BF16)
        return jnp.dot(h, w_ref[...], preferred_element_type=F32)

    def finish(s, acc):
        rows = slice(s * QKV_SUB, (s + 1) * QKV_SUB)
        vt_ref[0, :, rows] = acc[:, QK_WIDTH:].T.astype(BF16)
        cosv = cos_ref[rows, :]
        sinv = sin_ref[rows, :]
        for j in range(QK_WIDTH // MXU_COLS):
            y = acc[:, j * MXU_COLS:(j + 1) * MXU_COLS]
            ss = jnp.dot((y * y).astype(BF16), gsum_ref[...], preferred_element_type=F32)
            yn = (y * lax.rsqrt(ss * (1.0 / HEAD_DIM) + EPS)
                  * gains[:, j * MXU_COLS:(j + 1) * MXU_COLS])
            for t in range(MXU_COLS // LANES):
                ys = yn[:, t * LANES:(t + 1) * LANES]
                partner = jnp.where(even_lane,
                                    pltpu.roll(ys, LANES - 1, 1),
                                    pltpu.roll(ys, 1, 1))
                r = ys * cosv + partner * sinv
                col = j * MXU_COLS + t * LANES
                if col < Q_WIDTH:
                    qt_ref[0, col:col + LANES, rows] = r.T.astype(BF16)
                else:
                    k_ref[0, rows, col - Q_WIDTH:col - Q_WIDTH + LANES] = r.astype(BF16)

    acc_next = project(0)
    for s in range(n_sub):
        acc = acc_next
        if s + 1 < n_sub:
            acc_next = project(s + 1)
        finish(s, acc)


def _qkv_call(x, mod, layer, norm_g, w_qkv, attn_layer, gsum, qk_gain, cos_t, sin_t):
    batch, seq, _ = x.shape
    tm = QKV_TM
    return pl.pallas_call(
        _qkv_kernel,
        grid=(batch, seq // tm),
        in_specs=[
            pl.BlockSpec((1, tm, D_MODEL), lambda b, i: (b, i, 0)),
            _mod_spec(layer),
            _layer_slab(norm_g, layer),
            _layer_slab(w_qkv, attn_layer),
            _resident((MXU_COLS, MXU_COLS), lambda b, i: (0, 0)),
            _resident((1, QK_WIDTH), lambda b, i: (0, 0)),
            pl.BlockSpec((tm, LANES), lambda b, i: (i, 0)),
            pl.BlockSpec((tm, LANES), lambda b, i: (i, 0)),
        ],
        out_specs=[
            pl.BlockSpec((1, Q_WIDTH, tm), lambda b, i: (b, 0, i)),
            pl.BlockSpec((1, tm, KV_WIDTH), lambda b, i: (b, i, 0)),
            pl.BlockSpec((1, KV_WIDTH, tm), lambda b, i: (b, 0, i)),
        ],
        out_shape=[
            jax.ShapeDtypeStruct((batch, Q_WIDTH, seq), BF16),
            jax.ShapeDtypeStruct((batch, seq, KV_WIDTH), BF16),
            jax.ShapeDtypeStruct((batch, KV_WIDTH, seq), BF16),
        ],
        compiler_params=_cparams(2),
        name="attn_qkv",
    )(x, mod, norm_g, w_qkv, gsum, qk_gain, cos_t, sin_t)


ATTN_TQ = 512
ATTN_KC = 256
ATTN_SLOTS = 3
ONES_ROWS = 16
KV_PER_STEP = N_KV_HEADS
KV_PER_LANES = LANES // HEAD_DIM


def _attn_kernel(qt_ref, k_ref, vt_ref, o_ref, *st_scr):
    n_slots = len(st_scr)
    ahead = n_slots - 1
    seq = k_ref.shape[1]
    tq = qt_ref.shape[2]
    n_heads = KV_PER_STEP * KV_GROUP
    n_kc = seq // ATTN_KC
    zeros = jnp.zeros((HEAD_DIM, tq), BF16)
    ones = jnp.ones((ONES_ROWS, ATTN_KC), BF16)

    def q_operand(h):
        qt = qt_ref[0, h * HEAD_DIM:(h + 1) * HEAD_DIM, :]
        first = (h // KV_GROUP) % KV_PER_LANES == 0
        return jnp.concatenate([qt, zeros] if first else [zeros, qt], axis=0)

    def score_chunk(h, qt_pad, c):
        rows = slice(c * ATTN_KC, (c + 1) * ATTN_KC)
        blk = (h // KV_GROUP) // KV_PER_LANES
        k_blk = k_ref[0, rows, blk * LANES:(blk + 1) * LANES]
        st = jnp.dot(k_blk, qt_pad, preferred_element_type=F32)
        st_scr[h % n_slots][rows, :] = st
        return jnp.max(st, axis=0, keepdims=True)

    def reduce_max(parts):
        while len(parts) > 1:
            parts = [jnp.maximum(a, b) for a, b in zip(parts[0::2], parts[1::2])]
        return parts[0]

    maxes = {}
    for h in range(min(ahead, n_heads)):
        qt_pad = q_operand(h)
        maxes[h] = reduce_max([score_chunk(h, qt_pad, c) for c in range(n_kc)])
    done = []
    for h in range(n_heads):
        m = maxes.pop(h)
        g = h // KV_GROUP
        nxt = h + ahead
        if nxt < n_heads:
            qt_pad = q_operand(nxt)
        parts = []
        acc = jnp.zeros((HEAD_DIM + ONES_ROWS, tq), F32)
        for c in range(n_kc):
            rows = slice(c * ATTN_KC, (c + 1) * ATTN_KC)
            if nxt < n_heads:
                parts.append(score_chunk(nxt, qt_pad, c))
            pt = jnp.exp2(st_scr[h % n_slots][rows, :] - m).astype(BF16)
            vt_aug = jnp.concatenate([vt_ref[0, g * HEAD_DIM:(g + 1) * HEAD_DIM, rows], ones],
                                     axis=0)
            acc = acc + jnp.dot(vt_aug, pt, preferred_element_type=F32)
        if parts:
            maxes[nxt] = reduce_max(parts)
        done.append(acc[:HEAD_DIM] / acc[HEAD_DIM:HEAD_DIM + 1])
        if len(done) == 2:
            both = jnp.concatenate(done, axis=0)
            o_ref[0, :, (h - 1) * HEAD_DIM:(h + 1) * HEAD_DIM] = both.T.astype(BF16)
            done = []


def _attn_call(qt, k, vt):
    batch, seq, _ = k.shape
    tq = ATTN_TQ
    qw = KV_PER_STEP * KV_GROUP * HEAD_DIM
    return pl.pallas_call(
        _attn_kernel,
        grid=(batch, N_KV_HEADS // KV_PER_STEP, seq // tq),
        in_specs=[
            pl.BlockSpec((1, qw, tq), lambda b, j, i: (b, j, i)),
            pl.BlockSpec((1, seq, KV_PER_STEP * HEAD_DIM), lambda b, j, i: (b, 0, j)),
            pl.BlockSpec((1, KV_PER_STEP * HEAD_DIM, seq), lambda b, j, i: (b, j, 0)),
        ],
        out_specs=pl.BlockSpec((1, tq, qw), lambda b, j, i: (b, i, j)),
        out_shape=jax.ShapeDtypeStruct((batch, seq, Q_WIDTH), BF16),
        scratch_shapes=[pltpu.VMEM((seq, tq), F32)] * ATTN_SLOTS,
        compiler_params=_cparams(3),
        name="attn_core",
    )(qt, k, vt)


MLP_TM = 512
MLP_TH = 1024


def _mlp_body(xv, mod, g_ref, win_ref, wout_ref, out_ref, a_scr):
    shift = mod[:, 3 * D_MODEL:4 * D_MODEL]
    scale = mod[:, 4 * D_MODEL:5 * D_MODEL]
    gate = mod[:, 5 * D_MODEL:6 * D_MODEL]
    h = _modulated_rmsnorm(xv, g_ref[...], shift, scale).astype(BF16)
    for c in range(MLP_HIDDEN // MLP_TH):
        a = jnp.dot(h, win_ref[:, c * MLP_TH:(c + 1) * MLP_TH], preferred_element_type=F32)
        a = jnp.maximum(a, 0.0)
        a_scr[:, c * MLP_TH:(c + 1) * MLP_TH] = (a * a).astype(BF16)
    y = jnp.dot(a_scr[...], wout_ref[...], preferred_element_type=F32)
    out_ref[0] = xv + gate * y


def _mlp_kernel(x_ref, mod_ref, g_ref, win_ref, wout_ref, out_ref, a_scr):
    _mlp_body(x_ref[0], mod_ref[0], g_ref, win_ref, wout_ref, out_ref, a_scr)


def _proj_mlp_kernel(x_ref, o_ref, wo_ref, mod_ref, g_ref, win_ref, wout_ref, out_ref, a_scr):
    mod = mod_ref[0]
    gate1 = mod[:, 2 * D_MODEL:3 * D_MODEL]
    y = jnp.dot(o_ref[0], wo_ref[...], preferred_element_type=F32)
    _mlp_body(x_ref[0] + gate1 * y, mod, g_ref, win_ref, wout_ref, out_ref, a_scr)


def _mlp_call(x, mod, layer, norm_g, w_in, w_out, attn=None):
    batch, seq, _ = x.shape
    tm = MLP_TM
    tile = pl.BlockSpec((1, tm, D_MODEL), lambda b, i: (b, i, 0))
    specs = [_mod_spec(layer), _layer_slab(norm_g, layer), _layer_slab(w_in, layer),
             _layer_slab(w_out, layer)]
    args = (mod, norm_g, w_in, w_out)
    if attn is None:
        body, pre_specs, pre_args = _mlp_kernel, [tile], (x,)
    else:
        o, w_o, attn_layer = attn
        body = _proj_mlp_kernel
        pre_specs = [tile, pl.BlockSpec((1, tm, Q_WIDTH), lambda b, i: (b, i, 0)),
                     _layer_slab(w_o, attn_layer)]
        pre_args = (x, o, w_o)
    return pl.pallas_call(
        body,
        grid=(batch, seq // tm),
        in_specs=pre_specs + specs,
        out_specs=tile,
        out_shape=jax.ShapeDtypeStruct(x.shape, F32),
        scratch_shapes=[pltpu.VMEM((tm, MLP_HIDDEN), BF16)],
        compiler_params=_cparams(2),
        name="sq_relu_mlp" if attn is None else "proj_sq_relu_mlp",
    )(*pre_args, *args)


GMLP_TM = 512
GMLP_TN = 1024
PAIR_WIDTH = 2 * GMLP_GROUP_WIDTH


def _gelu(z):
    return 0.5 * z * (1.0 + lax.erf(z * (2.0 ** -0.5)))


def _gmlp_kernel(x_ref, mod_ref, g_ref, win_ref, bin_ref, lng_ref, lnb_ref, ws_ref, bs_ref,
                 wout_ref, out_ref, v_scr, u_scr, vn_scr, uv_scr):
    mod = mod_ref[0]
    shift = mod[:, 0:D_MODEL]
    scale = mod[:, D_MODEL:2 * D_MODEL]
    gate = mod[:, 2 * D_MODEL:3 * D_MODEL]
    xv = x_ref[0]
    tm = xv.shape[0]
    h = _modulated_rmsnorm(xv, g_ref[...], shift, scale).astype(BF16)
    n_col = GMLP_HALF // GMLP_TN

    s1 = jnp.zeros((tm, 1), F32)
    s2 = jnp.zeros((tm, 1), F32)
    pivot = None
    for c in range(n_col):
        lo = GMLP_HALF + c * GMLP_TN
        z = jnp.dot(h, win_ref[:, lo:lo + GMLP_TN], preferred_element_type=F32)
        zg = _gelu(z + bin_ref[:, lo:lo + GMLP_TN])
        v_scr[:, c * GMLP_TN:(c + 1) * GMLP_TN] = zg
        if pivot is None:
            pivot = jnp.sum(zg, axis=-1, keepdims=True) * (1.0 / GMLP_TN)
        d = zg - pivot
        s1 += jnp.sum(d, axis=-1, keepdims=True)
        s2 += jnp.sum(d * d, axis=-1, keepdims=True)
    for c in range(n_col):
        sl = slice(c * GMLP_TN, (c + 1) * GMLP_TN)
        z = jnp.dot(h, win_ref[:, sl], preferred_element_type=F32)
        u_scr[:, sl] = _gelu(z + bin_ref[:, sl])
    off = s1 * (1.0 / GMLP_HALF)
    mu = pivot + off
    rstd = lax.rsqrt(s2 * (1.0 / GMLP_HALF) - off * off + EPS)
    for c in range(n_col):
        sl = slice(c * GMLP_TN, (c + 1) * GMLP_TN)
        vn = (v_scr[:, sl] - mu) * rstd * lng_ref[:, sl] + lnb_ref[:, sl]
        vn_scr[:, sl] = vn.astype(BF16)

    lane = lax.broadcasted_iota(jnp.int32, (GMLP_CHUNK, LANES), 1)
    low_lanes = lane < (GMLP_GROUP_WIDTH - LANES)
    for n in range(tm // GMLP_CHUNK):
        rows = slice(n * GMLP_CHUNK, (n + 1) * GMLP_CHUNK)
        for j in range(GMLP_GROUPS // 2):
            base = j * PAIR_WIDTH
            va = vn_scr[rows, base:base + LANES]
            vc = vn_scr[rows, base + LANES:base + 2 * LANES]
            vb = vn_scr[rows, base + 2 * LANES:base + 3 * LANES]
            ra = jnp.dot(ws_ref[2 * j], jnp.concatenate([va, vc], axis=1),
                         preferred_element_type=F32)
            rb = jnp.dot(ws_ref[2 * j + 1], jnp.concatenate([vb, vc], axis=1),
                         preferred_element_type=F32)
            rc = jnp.where(low_lanes, ra[:, LANES:], rb[:, LANES:])
            v_scr[rows, base:base + LANES] = ra[:, :LANES] + bs_ref[:, base:base + LANES]
            v_scr[rows, base + LANES:base + 2 * LANES] = (
                rc + bs_ref[:, base + LANES:base + 2 * LANES])
            v_scr[rows, base + 2 * LANES:base + 3 * LANES] = (
                rb[:, :LANES] + bs_ref[:, base + 2 * LANES:base + 3 * LANES])

    for c in range(n_col):
        sl = slice(c * GMLP_TN, (c + 1) * GMLP_TN)
        uv_scr[:, sl] = (u_scr[:, sl] * v_scr[:, sl]).astype(BF16)
    y = jnp.dot(uv_scr[...], wout_ref[...], preferred_element_type=F32)
    out_ref[0] = xv + gate * y


def _gmlp_call(x, mod, layer, norm_g, gmlp_layer, w_in, b_in, ln_g, ln_b, w_s, bs_table, w_out):
    batch, seq, _ = x.shape
    tm = GMLP_TM
    gmlp_params = (w_in, b_in, ln_g, ln_b, w_s, bs_table, w_out)
    return pl.pallas_call(
        _gmlp_kernel,
        grid=(batch, seq // tm),
        in_specs=[
            pl.BlockSpec((1, tm, D_MODEL), lambda b, i: (b, i, 0)),
            _mod_spec(layer),
            _layer_slab(norm_g, layer),
        ] + [_layer_slab(p, gmlp_layer) for p in gmlp_params],
        out_specs=pl.BlockSpec((1, tm, D_MODEL), lambda b, i: (b, i, 0)),
        out_shape=jax.ShapeDtypeStruct(x.shape, F32),
        scratch_shapes=[
            pltpu.VMEM((tm, GMLP_HALF), F32),
            pltpu.VMEM((tm, GMLP_HALF), F32),
            pltpu.VMEM((tm, GMLP_HALF), BF16),
            pltpu.VMEM((tm, GMLP_HALF), BF16),
        ],
        compiler_params=_cparams(2),
        name="gmlp_mixer",
    )(x, mod, norm_g, *gmlp_params)


def _rope_tables(seq_len):
    t = jnp.arange(seq_len, dtype=jnp.int32)
    rows = seq_len // GRID_W
    row = (t // GRID_W - rows // 2).astype(F32)
    col = (t % GRID_W - GRID_W // 2).astype(F32)
    inv_freq = ROPE_THETA ** (-jnp.arange(ROPE_PAIRS_AXIS, dtype=F32) / ROPE_PAIRS_AXIS)
    ang = jnp.concatenate([row[:, None] * inv_freq, col[:, None] * inv_freq], axis=-1)
    reps = LANES // HEAD_DIM
    cos_t = jnp.tile(jnp.repeat(jnp.cos(ang), 2, axis=-1), (1, reps))
    sin_t = jnp.tile(jnp.repeat(jnp.sin(ang), 2, axis=-1), (1, reps))
    sign = jnp.tile(jnp.asarray([-1.0, 1.0], F32), LANES // 2)
    return cos_t, sin_t * sign


def kernel(x, c, ada_w, ada_b, norm1_g, norm2_g, attn_w_qkv, attn_q_norm_g, attn_k_norm_g,
           attn_w_o, gmlp_w_in, gmlp_b_in, gmlp_ln_g, gmlp_ln_b, gmlp_w_s, gmlp_b_s,
           gmlp_w_out, mlp_w_in, mlp_w_out):
    batch, seq, _ = x.shape
    mod = _mod_call(c, ada_w, ada_b)
    cos_t, sin_t = _rope_tables(seq)
    head_ids = jnp.arange(MXU_COLS) // HEAD_DIM
    gsum = (head_ids[:, None] == head_ids[None, :]).astype(BF16)
    q_scale = HEAD_DIM ** -0.5 * math.log2(math.e)

    mod = mod.reshape(DEPTH, batch, 1, N_MOD * D_MODEL)
    mlp_w_in, mlp_w_out = mlp_w_in.astype(BF16), mlp_w_out.astype(BF16)
    attn_w_qkv, attn_w_o = attn_w_qkv.astype(BF16), attn_w_o.astype(BF16)
    gmlp_w_in, gmlp_w_out = gmlp_w_in.astype(BF16), gmlp_w_out.astype(BF16)
    gmlp_w_s = gmlp_w_s.astype(BF16)
    n1 = norm1_g[:, None, :]
    n2 = norm2_g[:, None, :]
    b_in = gmlp_b_in[:, None, :]
    ln_g = gmlp_ln_g[:, None, :]
    ln_b = gmlp_ln_b[:, None, :]
    bs_table = jnp.repeat(jnp.swapaxes(gmlp_b_s, 1, 2), GMLP_GROUP_WIDTH, axis=2)

    for i in range(DEPTH):
        j = i // 2
        if i % 2 == 0:
            qk_gain = jnp.concatenate([jnp.tile(attn_q_norm_g[j] * q_scale, N_HEADS),
                                       jnp.tile(attn_k_norm_g[j], N_KV_HEADS)])
            qt, k, vt = _qkv_call(x, mod, i, n1, attn_w_qkv, j, gsum,
                                  qk_gain.reshape(1, QK_WIDTH), cos_t, sin_t)
            o = _attn_call(qt, k, vt)
            x = _mlp_call(x, mod, i, n2, mlp_w_in, mlp_w_out, attn=(o, attn_w_o, j))
        else:
            x = _gmlp_call(x, mod, i, n1, j, gmlp_w_in, b_in, ln_g, ln_b, gmlp_w_s, bs_table,
                           gmlp_w_out)
            x = _mlp_call(x, mod, i, n2, mlp_w_in, mlp_w_out)
    return x
```

```python
import math

import jax
import jax.numpy as jnp
from jax import lax
from jax.experimental import pallas as pl
from jax.experimental.pallas import tpu as pltpu

D_MODEL = 1024
DEPTH = 4
HEAD_DIM = 64
N_HEADS = 16
N_KV_HEADS = 4
KV_GROUP = N_HEADS // N_KV_HEADS
Q_WIDTH = N_HEADS * HEAD_DIM
KV_WIDTH = N_KV_HEADS * HEAD_DIM
QK_WIDTH = Q_WIDTH + KV_WIDTH
QKV_WIDTH = Q_WIDTH + 2 * KV_WIDTH
ROPE_THETA = 10000.0
ROPE_PAIRS_AXIS = HEAD_DIM // 4
GRID_W = 64
GMLP_FFN = 6 * D_MODEL
GMLP_HALF = GMLP_FFN // 2
GMLP_CHUNK = 128
GMLP_GROUPS = 16
GMLP_GROUP_WIDTH = GMLP_HALF // GMLP_GROUPS
MLP_HIDDEN = 4 * D_MODEL
N_MOD = 6
EPS = 1e-6

LANES = 128
MXU_COLS = 256
VMEM_LIMIT = 56 * 1024 * 1024

F32 = jnp.float32
BF16 = jnp.bfloat16


def _cparams(n_axes):
    return pltpu.CompilerParams(
        dimension_semantics=("arbitrary",) * n_axes,
        vmem_limit_bytes=VMEM_LIMIT,
    )


def _resident(block_shape, index_map):
    return pl.BlockSpec(block_shape, index_map, pipeline_mode=pl.Buffered(1))


def _layer_slab(stacked, layer):
    tail = stacked.shape[1:]
    return _resident((None,) + tail, lambda b, i: (layer,) + (0,) * len(tail))


def _mod_spec(layer):
    return pl.BlockSpec((None, 1, 1, N_MOD * D_MODEL), lambda b, i: (layer, b, 0, 0))


def _split_bf16(a):
    hi = a.astype(BF16)
    lo = (a - hi.astype(F32)).astype(BF16)
    return hi, lo


def _modulated_rmsnorm(xv, gain, shift, scale):
    ms = jnp.mean(xv * xv, axis=-1, keepdims=True)
    y = xv * lax.rsqrt(ms + EPS) * gain
    return y * (1.0 + scale) + shift


MOD_TN = 1536


def _mod_kernel(c_ref, w_ref, b_ref, o_ref):
    cv = c_ref[...]
    cond = cv * jax.nn.sigmoid(cv)
    c_hi, c_lo = _split_bf16(cond)
    w_hi, w_lo = _split_bf16(w_ref[0])
    acc = jnp.dot(c_hi, w_hi, preferred_element_type=F32)
    acc += jnp.dot(c_lo, w_hi, preferred_element_type=F32)
    acc += jnp.dot(c_hi, w_lo, preferred_element_type=F32)
    o_ref[0] = acc + b_ref[0]


def _mod_call(c, ada_w, ada_b):
    batch = c.shape[0]
    n_out = N_MOD * D_MODEL
    return pl.pallas_call(
        _mod_kernel,
        grid=(DEPTH, n_out // MOD_TN),
        in_specs=[
            pl.BlockSpec((batch, D_MODEL), lambda l, j: (0, 0)),
            pl.BlockSpec((1, D_MODEL, MOD_TN), lambda l, j: (l, 0, j)),
            pl.BlockSpec((1, 1, MOD_TN), lambda l, j: (l, 0, j)),
        ],
        out_specs=pl.BlockSpec((1, batch, MOD_TN), lambda l, j: (l, 0, j)),
        out_shape=jax.ShapeDtypeStruct((DEPTH, batch, n_out), F32),
        compiler_params=_cparams(2),
        name="adaln_mod",
    )(c, ada_w, ada_b.reshape(DEPTH, 1, n_out))


QKV_TM = 2048
QKV_SUB = 256


def _qkv_kernel(x_ref, mod_ref, g_ref, w_ref, gsum_ref, qkg_ref, cos_ref, sin_ref,
                qt_ref, k_ref, vt_ref):
    mod = mod_ref[0]
    shift = mod[:, 0:D_MODEL]
    scale = mod[:, D_MODEL:2 * D_MODEL]
    gains = qkg_ref[...]
    lane = lax.broadcasted_iota(jnp.int32, (QKV_SUB, LANES), 1)
    even_lane = (lane & 1) == 0
    n_sub = x_ref.shape[1] // QKV_SUB

    def project(s):
        rows = slice(s * QKV_SUB, (s + 1) * QKV_SUB)
        h = _modulated_rmsnorm(x_ref[0, rows, :], g_ref[...], shift, scale).astype(BF16)
        return jnp.dot(h, w_ref[...], preferred_element_type=F32)

    def finish(s, acc):
        rows = slice(s * QKV_SUB, (s + 1) * QKV_SUB)
        vt_ref[0, :, rows] = acc[:, QK_WIDTH:].T.astype(BF16)
        cosv = cos_ref[rows, :]
        sinv = sin_ref[rows, :]
        for j in range(QK_WIDTH // MXU_COLS):
            y = acc[:, j * MXU_COLS:(j + 1) * MXU_COLS]
            ss = jnp.dot((y * y).astype(BF16), gsum_ref[...], preferred_element_type=F32)
            yn = (y * lax.rsqrt(ss * (1.0 / HEAD_DIM) + EPS)
                  * gains[:, j * MXU_COLS:(j + 1) * MXU_COLS])
            for t in range(MXU_COLS // LANES):
                ys = yn[:, t * LANES:(t + 1) * LANES]
                partner = jnp.where(even_lane,
                                    pltpu.roll(ys, LANES - 1, 1),
                                    pltpu.roll(ys, 1, 1))
                r = ys * cosv + partner * sinv
                col = j * MXU_COLS + t * LANES
                if col < Q_WIDTH:
                    qt_ref[0, col:col + LANES, rows] = r.T.astype(BF16)
                else:
                    k_ref[0, rows, col - Q_WIDTH:col - Q_WIDTH + LANES] = r.astype(BF16)

    acc_next = project(0)
    for s in range(n_sub):
        acc = acc_next
        if s + 1 < n_sub:
            acc_next = project(s + 1)
        finish(s, acc)


def _qkv_call(x, mod, layer, norm_g, w_qkv, attn_layer, gsum, qk_gain, cos_t, sin_t):
    batch, seq, _ = x.shape
    tm = QKV_TM
    return pl.pallas_call(
        _qkv_kernel,
        grid=(batch, seq // tm),
        in_specs=[
            pl.BlockSpec((1, tm, D_MODEL), lambda b, i: (b, i, 0)),
            _mod_spec(layer),
            _layer_slab(norm_g, layer),
            _layer_slab(w_qkv, attn_layer),
            _resident((MXU_COLS, MXU_COLS), lambda b, i: (0, 0)),
            _resident((1, QK_WIDTH), lambda b, i: (0, 0)),
            pl.BlockSpec((tm, LANES), lambda b, i: (i, 0)),
            pl.BlockSpec((tm, LANES), lambda b, i: (i, 0)),
        ],
        out_specs=[
            pl.BlockSpec((1, Q_WIDTH, tm), lambda b, i: (b, 0, i)),
            pl.BlockSpec((1, tm, KV_WIDTH), lambda b, i: (b, i, 0)),
            pl.BlockSpec((1, KV_WIDTH, tm), lambda b, i: (b, 0, i)),
        ],
        out_shape=[
            jax.ShapeDtypeStruct((batch, Q_WIDTH, seq), BF16),
            jax.ShapeDtypeStruct((batch, seq, KV_WIDTH), BF16),
            jax.ShapeDtypeStruct((batch, KV_WIDTH, seq), BF16),
        ],
        compiler_params=_cparams(2),
        name="attn_qkv",
    )(x, mod, norm_g, w_qkv, gsum, qk_gain, cos_t, sin_t)


ATTN_TQ = 512
ATTN_KC = 256
ATTN_SLOTS = 3
ONES_ROWS = 16
KV_PER_STEP = N_KV_HEADS
KV_PER_LANES = LANES // HEAD_DIM


def _attn_kernel(qt_ref, k_ref, vt_ref, o_ref, *st_scr):
    n_slots = len(st_scr)
    ahead = n_slots - 1
    seq = k_ref.shape[1]
    tq = qt_ref.shape[2]
    n_heads = KV_PER_STEP * KV_GROUP
    n_kc = seq // ATTN_KC
    zeros = jnp.zeros((HEAD_DIM, tq), BF16)
    ones = jnp.ones((ONES_ROWS, ATTN_KC), BF16)

    def q_operand(h):
        qt = qt_ref[0, h * HEAD_DIM:(h + 1) * HEAD_DIM, :]
        first = (h // KV_GROUP) % KV_PER_LANES == 0
        return jnp.concatenate([qt, zeros] if first else [zeros, qt], axis=0)

    def score_chunk(h, qt_pad, c):
        rows = slice(c * ATTN_KC, (c + 1) * ATTN_KC)
        blk = (h // KV_GROUP) // KV_PER_LANES
        k_blk = k_ref[0, rows, blk * LANES:(blk + 1) * LANES]
        st = jnp.dot(k_blk, qt_pad, preferred_element_type=F32)
        st_scr[h % n_slots][rows, :] = st
        return jnp.max(st, axis=0, keepdims=True)

    def reduce_max(parts):
        while len(parts) > 1:
            parts = [jnp.maximum(a, b) for a, b in zip(parts[0::2], parts[1::2])]
        return parts[0]

    maxes = {}
    for h in range(min(ahead, n_heads)):
        qt_pad = q_operand(h)
        maxes[h] = reduce_max([score_chunk(h, qt_pad, c) for c in range(n_kc)])
    done = []
    for h in range(n_heads):
        m = maxes.pop(h)
        g = h // KV_GROUP
        nxt = h + ahead
        if nxt < n_heads:
            qt_pad = q_operand(nxt)
        parts = []
        acc = jnp.zeros((HEAD_DIM + ONES_ROWS, tq), F32)
        for c in range(n_kc):
            rows = slice(c * ATTN_KC, (c + 1) * ATTN_KC)
            if nxt < n_heads:
                parts.append(score_chunk(nxt, qt_pad, c))
            pt = jnp.exp2(st_scr[h % n_slots][rows, :] - m).astype(BF16)
            vt_aug = jnp.concatenate([vt_ref[0, g * HEAD_DIM:(g + 1) * HEAD_DIM, rows], ones],
                                     axis=0)
            acc = acc + jnp.dot(vt_aug, pt, preferred_element_type=F32)
        if parts:
            maxes[nxt] = reduce_max(parts)
        done.append(acc[:HEAD_DIM] / acc[HEAD_DIM:HEAD_DIM + 1])
        if len(done) == 2:
            both = jnp.concatenate(done, axis=0)
            o_ref[0, :, (h - 1) * HEAD_DIM:(h + 1) * HEAD_DIM] = both.T.astype(BF16)
            done = []


def _attn_call(qt, k, vt):
    batch, seq, _ = k.shape
    tq = ATTN_TQ
    qw = KV_PER_STEP * KV_GROUP * HEAD_DIM
    return pl.pallas_call(
        _attn_kernel,
        grid=(batch, N_KV_HEADS // KV_PER_STEP, seq // tq),
        in_specs=[
            pl.BlockSpec((1, qw, tq), lambda b, j, i: (b, j, i)),
            pl.BlockSpec((1, seq, KV_PER_STEP * HEAD_DIM), lambda b, j, i: (b, 0, j)),
            pl.BlockSpec((1, KV_PER_STEP * HEAD_DIM, seq), lambda b, j, i: (b, j, 0)),
        ],
        out_specs=pl.BlockSpec((1, tq, qw), lambda b, j, i: (b, i, j)),
        out_shape=jax.ShapeDtypeStruct((batch, seq, Q_WIDTH), BF16),
        scratch_shapes=[pltpu.VMEM((seq, tq), F32)] * ATTN_SLOTS,
        compiler_params=_cparams(3),
        name="attn_core",
    )(qt, k, vt)


MLP_TM = 1024
MLP_TH = 1024


def _mlp_body(xv, mod, g_ref, win_ref, wout_ref, out_ref, a_scr):
    shift = mod[:, 3 * D_MODEL:4 * D_MODEL]
    scale = mod[:, 4 * D_MODEL:5 * D_MODEL]
    gate = mod[:, 5 * D_MODEL:6 * D_MODEL]
    h = _modulated_rmsnorm(xv, g_ref[...], shift, scale).astype(BF16)
    for c in range(MLP_HIDDEN // MLP_TH):
        a = jnp.dot(h, win_ref[:, c * MLP_TH:(c + 1) * MLP_TH], preferred_element_type=F32)
        a = jnp.maximum(a, 0.0)
        a_scr[:, c * MLP_TH:(c + 1) * MLP_TH] = (a * a).astype(BF16)
    y = jnp.dot(a_scr[...], wout_ref[...], preferred_element_type=F32)
    out_ref[0] = xv + gate * y


def _mlp_kernel(x_ref, mod_ref, g_ref, win_ref, wout_ref, out_ref, a_scr):
    _mlp_body(x_ref[0], mod_ref[0], g_ref, win_ref, wout_ref, out_ref, a_scr)


def _proj_mlp_kernel(x_ref, o_ref, wo_ref, mod_ref, g_ref, win_ref, wout_ref, out_ref, a_scr):
    mod = mod_ref[0]
    gate1 = mod[:, 2 * D_MODEL:3 * D_MODEL]
    y = jnp.dot(o_ref[0], wo_ref[...], preferred_element_type=F32)
    _mlp_body(x_ref[0] + gate1 * y, mod, g_ref, win_ref, wout_ref, out_ref, a_scr)


def _mlp_call(x, mod, layer, norm_g, w_in, w_out, attn=None):
    batch, seq, _ = x.shape
    tm = MLP_TM
    tile = pl.BlockSpec((1, tm, D_MODEL), lambda b, i: (b, i, 0))
    specs = [_mod_spec(layer), _layer_slab(norm_g, layer), _layer_slab(w_in, layer),
             _layer_slab(w_out, layer)]
    args = (mod, norm_g, w_in, w_out)
    if attn is None:
        body, pre_specs, pre_args = _mlp_kernel, [tile], (x,)
    else:
        o, w_o, attn_layer = attn
        body = _proj_mlp_kernel
        pre_specs = [tile, pl.BlockSpec((1, tm, Q_WIDTH), lambda b, i: (b, i, 0)),
                     _layer_slab(w_o, attn_layer)]
        pre_args = (x, o, w_o)
    return pl.pallas_call(
        body,
        grid=(batch, seq // tm),
        in_specs=pre_specs + specs,
        out_specs=tile,
        out_shape=jax.ShapeDtypeStruct(x.shape, F32),
        scratch_shapes=[pltpu.VMEM((tm, MLP_HIDDEN), BF16)],
        compiler_params=_cparams(2),
        name="sq_relu_mlp" if attn is None else "proj_sq_relu_mlp",
    )(*pre_args, *args)


GMLP_TM = 512
GMLP_TN = 512
PAIR_WIDTH = 2 * GMLP_GROUP_WIDTH


def _gelu(z):
    return 0.5 * z * (1.0 + lax.erf(z * (2.0 ** -0.5)))


def _gmlp_kernel(x_ref, mod_ref, g_ref, win_ref, bin_ref, lng_ref, lnb_ref, ws_ref, bs_ref,
                 wout_ref, out_ref, v_scr, u_scr, vn_scr, uv_scr):
    mod = mod_ref[0]
    shift = mod[:, 0:D_MODEL]
    scale = mod[:, D_MODEL:2 * D_MODEL]
    gate = mod[:, 2 * D_MODEL:3 * D_MODEL]
    xv = x_ref[0]
    tm = xv.shape[0]
    h = _modulated_rmsnorm(xv, g_ref[...], shift, scale).astype(BF16)
    n_col = GMLP_HALF // GMLP_TN

    s1 = jnp.zeros((tm, 1), F32)
    s2 = jnp.zeros((tm, 1), F32)
    pivot = None
    for c in range(n_col):
        lo = GMLP_HALF + c * GMLP_TN
        z = jnp.dot(h, win_ref[:, lo:lo + GMLP_TN], preferred_element_type=F32)
        zg = _gelu(z + bin_ref[:, lo:lo + GMLP_TN])
        v_scr[:, c * GMLP_TN:(c + 1) * GMLP_TN] = zg
        if pivot is None:
            pivot = jnp.sum(zg, axis=-1, keepdims=True) * (1.0 / GMLP_TN)
        d = zg - pivot
        s1 += jnp.sum(d, axis=-1, keepdims=True)
        s2 += jnp.sum(d * d, axis=-1, keepdims=True)
    for c in range(n_col):
        sl = slice(c * GMLP_TN, (c + 1) * GMLP_TN)
        z = jnp.dot(h, win_ref[:, sl], preferred_element_type=F32)
        u_scr[:, sl] = _gelu(z + bin_ref[:, sl])
    off = s1 * (1.0 / GMLP_HALF)
    mu = pivot + off
    rstd = lax.rsqrt(s2 * (1.0 / GMLP_HALF) - off * off + EPS)
    lane = lax.broadcasted_iota(jnp.int32, (GMLP_CHUNK, LANES), 1)
    low_lanes = lane < (GMLP_GROUP_WIDTH - LANES)
    n_rc = tm // GMLP_CHUNK
    for n in range(n_rc):
        rows = slice(n * GMLP_CHUNK, (n + 1) * GMLP_CHUNK)
        vn = (v_scr[rows, :] - mu[rows]) * rstd[rows] * lng_ref[...] + lnb_ref[...]
        vn_scr[rows, :] = vn.astype(BF16)
        for j in range(GMLP_GROUPS // 2):
            base = j * PAIR_WIDTH
            va = vn_scr[rows, base:base + LANES]
            vc = vn_scr[rows, base + LANES:base + 2 * LANES]
            vb = vn_scr[rows, base + 2 * LANES:base + 3 * LANES]
            ra = jnp.dot(ws_ref[2 * j], jnp.concatenate([va, vc], axis=1),
                         preferred_element_type=F32)
            rb = jnp.dot(ws_ref[2 * j + 1], jnp.concatenate([vb, vc], axis=1),
                         preferred_element_type=F32)
            rc = jnp.where(low_lanes, ra[:, LANES:], rb[:, LANES:])
            v_scr[rows, base:base + LANES] = ra[:, :LANES] + bs_ref[:, base:base + LANES]
            v_scr[rows, base + LANES:base + 2 * LANES] = (
                rc + bs_ref[:, base + LANES:base + 2 * LANES])
            v_scr[rows, base + 2 * LANES:base + 3 * LANES] = (
                rb[:, :LANES] + bs_ref[:, base + 2 * LANES:base + 3 * LANES])
        uv_scr[rows, :] = (u_scr[rows, :] * v_scr[rows, :]).astype(BF16)

    y = jnp.dot(uv_scr[...], wout_ref[...], preferred_element_type=F32)
    out_ref[0] = xv + gate * y


def _gmlp_call(x, mod, layer, norm_g, gmlp_layer, w_in, b_in, ln_g, ln_b, w_s, bs_table, w_out):
    batch, seq, _ = x.shape
    tm = GMLP_TM
    gmlp_params = (w_in, b_in, ln_g, ln_b, w_s, bs_table, w_out)
    return pl.pallas_call(
        _gmlp_kernel,
        grid=(batch, seq // tm),
        in_specs=[
            pl.BlockSpec((1, tm, D_MODEL), lambda b, i: (b, i, 0)),
            _mod_spec(layer),
            _layer_slab(norm_g, layer),
        ] + [_layer_slab(p, gmlp_layer) for p in gmlp_params],
        out_specs=pl.BlockSpec((1, tm, D_MODEL), lambda b, i: (b, i, 0)),
        out_shape=jax.ShapeDtypeStruct(x.shape, F32),
        scratch_shapes=[
            pltpu.VMEM((tm, GMLP_HALF), F32),
            pltpu.VMEM((tm, GMLP_HALF), F32),
            pltpu.VMEM((tm, GMLP_HALF), BF16),
            pltpu.VMEM((tm, GMLP_HALF), BF16),
        ],
        compiler_params=_cparams(2),
        name="gmlp_mixer",
    )(x, mod, norm_g, *gmlp_params)


def _rope_tables(seq_len):
    t = jnp.arange(seq_len, dtype=jnp.int32)
    rows = seq_len // GRID_W
    row = (t // GRID_W - rows // 2).astype(F32)
    col = (t % GRID_W - GRID_W // 2).astype(F32)
    inv_freq = ROPE_THETA ** (-jnp.arange(ROPE_PAIRS_AXIS, dtype=F32) / ROPE_PAIRS_AXIS)
    ang = jnp.concatenate([row[:, None] * inv_freq, col[:, None] * inv_freq], axis=-1)
    reps = LANES // HEAD_DIM
    cos_t = jnp.tile(jnp.repeat(jnp.cos(ang), 2, axis=-1), (1, reps))
    sin_t = jnp.tile(jnp.repeat(jnp.sin(ang), 2, axis=-1), (1, reps))
    sign = jnp.tile(jnp.asarray([-1.0, 1.0], F32), LANES // 2)
    return cos_t, sin_t * sign


def kernel(x, c, ada_w, ada_b, norm1_g, norm2_g, attn_w_qkv, attn_q_norm_g, attn_k_norm_g,
           attn_w_o, gmlp_w_in, gmlp_b_in, gmlp_ln_g, gmlp_ln_b, gmlp_w_s, gmlp_b_s,
           gmlp_w_out, mlp_w_in, mlp_w_out):
    batch, seq, _ = x.shape
    mod = _mod_call(c, ada_w, ada_b)
    cos_t, sin_t = _rope_tables(seq)
    head_ids = jnp.arange(MXU_COLS) // HEAD_DIM
    gsum = (head_ids[:, None] == head_ids[None, :]).astype(BF16)
    q_scale = HEAD_DIM ** -0.5 * math.log2(math.e)

    mod = mod.reshape(DEPTH, batch, 1, N_MOD * D_MODEL)
    mlp_w_in, mlp_w_out = mlp_w_in.astype(BF16), mlp_w_out.astype(BF16)
    attn_w_qkv, attn_w_o = attn_w_qkv.astype(BF16), attn_w_o.astype(BF16)
    gmlp_w_in, gmlp_w_out = gmlp_w_in.astype(BF16), gmlp_w_out.astype(BF16)
    gmlp_w_s = gmlp_w_s.astype(BF16)
    n1 = norm1_g[:, None, :]
    n2 = norm2_g[:, None, :]
    b_in = gmlp_b_in[:, None, :]
    ln_g = gmlp_ln_g[:, None, :]
    ln_b = gmlp_ln_b[:, None, :]
    bs_table = jnp.repeat(jnp.swapaxes(gmlp_b_s, 1, 2), GMLP_GROUP_WIDTH, axis=2)

    for i in range(DEPTH):
        j = i // 2
        if i % 2 == 0:
            qk_gain = jnp.concatenate([jnp.tile(attn_q_norm_g[j] * q_scale, N_HEADS),
                                       jnp.tile(attn_k_norm_g[j], N_KV_HEADS)])
            qt, k, vt = _qkv_call(x, mod, i, n1, attn_w_qkv, j, gsum,
                                  qk_gain.reshape(1, QK_WIDTH), cos_t, sin_t)
            o = _attn_call(qt, k, vt)
            x = _mlp_call(x, mod, i, n2, mlp_w_in, mlp_w_out, attn=(o, attn_w_o, j))
        else:
            x = _gmlp_call(x, mod, i, n1, j, gmlp_w_in, b_in, ln_g, ln_b, gmlp_w_s, bs_table,
                           gmlp_w_out)
            x = _mlp_call(x, mod, i, n2, mlp_w_in, mlp_w_out)
    return x
```

```python
import math

import jax
import jax.numpy as jnp
from jax import lax
from jax.experimental import pallas as pl
from jax.experimental.pallas import tpu as pltpu

D_MODEL = 1024
DEPTH = 4
HEAD_DIM = 64
N_HEADS = 16
N_KV_HEADS = 4
KV_GROUP = N_HEADS // N_KV_HEADS
Q_WIDTH = N_HEADS * HEAD_DIM
KV_WIDTH = N_KV_HEADS * HEAD_DIM
QK_WIDTH = Q_WIDTH + KV_WIDTH
QKV_WIDTH = Q_WIDTH + 2 * KV_WIDTH
ROPE_THETA = 10000.0
ROPE_PAIRS_AXIS = HEAD_DIM // 4
GRID_W = 64
GMLP_FFN = 6 * D_MODEL
GMLP_HALF = GMLP_FFN // 2
GMLP_CHUNK = 128
GMLP_GROUPS = 16
GMLP_GROUP_WIDTH = GMLP_HALF // GMLP_GROUPS
MLP_HIDDEN = 4 * D_MODEL
N_MOD = 6
EPS = 1e-6

LANES = 128
MXU_COLS = 256
VMEM_LIMIT = 56 * 1024 * 1024

F32 = jnp.float32
BF16 = jnp.bfloat16


def _cparams(n_axes):
    return pltpu.CompilerParams(
        dimension_semantics=("arbitrary",) * n_axes,
        vmem_limit_bytes=VMEM_LIMIT,
    )


def _resident(block_shape, index_map):
    return pl.BlockSpec(block_shape, index_map, pipeline_mode=pl.Buffered(1))


def _layer_slab(stacked, layer):
    tail = stacked.shape[1:]
    return _resident((None,) + tail, lambda b, i: (layer,) + (0,) * len(tail))


def _mod_spec(layer):
    return pl.BlockSpec((None, 1, 1, N_MOD * D_MODEL), lambda b, i: (layer, b, 0, 0))


def _split_bf16(a):
    hi = a.astype(BF16)
    lo = (a - hi.astype(F32)).astype(BF16)
    return hi, lo


def _modulated_rmsnorm(xv, gain, shift, scale):
    ms = jnp.mean(xv * xv, axis=-1, keepdims=True)
    y = xv * lax.rsqrt(ms + EPS) * gain
    return y * (1.0 + scale) + shift


MOD_TN = 3072


def _mod_kernel(c_ref, w_ref, b_ref, o_ref):
    cv = c_ref[...]
    cond = cv * jax.nn.sigmoid(cv)
    c_hi, c_lo = _split_bf16(cond)
    w_hi, w_lo = _split_bf16(w_ref[0])
    acc = jnp.dot(c_hi, w_hi, preferred_element_type=F32)
    acc += jnp.dot(c_lo, w_hi, preferred_element_type=F32)
    acc += jnp.dot(c_hi, w_lo, preferred_element_type=F32)
    o_ref[0] = acc + b_ref[0]


def _mod_call(c, ada_w, ada_b):
    batch = c.shape[0]
    n_out = N_MOD * D_MODEL
    return pl.pallas_call(
        _mod_kernel,
        grid=(DEPTH, n_out // MOD_TN),
        in_specs=[
            pl.BlockSpec((batch, D_MODEL), lambda l, j: (0, 0)),
            pl.BlockSpec((1, D_MODEL, MOD_TN), lambda l, j: (l, 0, j)),
            pl.BlockSpec((1, 1, MOD_TN), lambda l, j: (l, 0, j)),
        ],
        out_specs=pl.BlockSpec((1, batch, MOD_TN), lambda l, j: (l, 0, j)),
        out_shape=jax.ShapeDtypeStruct((DEPTH, batch, n_out), F32),
        compiler_params=_cparams(2),
        name="adaln_mod",
    )(c, ada_w, ada_b.reshape(DEPTH, 1, n_out))


QKV_TM = 2048
QKV_SUB = 256
QKV_AHEAD = 1


def _qkv_kernel(x_ref, mod_ref, g_ref, w_ref, gsum_ref, qkg_ref, cos_ref, sin_ref,
                qt_ref, k_ref, vt_ref):
    mod = mod_ref[0]
    shift = mod[:, 0:D_MODEL]
    scale = mod[:, D_MODEL:2 * D_MODEL]
    gains = qkg_ref[...]
    lane = lax.broadcasted_iota(jnp.int32, (QKV_SUB, LANES), 1)
    even_lane = (lane & 1) == 0
    n_sub = x_ref.shape[1] // QKV_SUB

    def project(s):
        rows = slice(s * QKV_SUB, (s + 1) * QKV_SUB)
        h = _modulated_rmsnorm(x_ref[0, rows, :], g_ref[...], shift, scale).astype(BF16)
        return jnp.dot(h, w_ref[...], preferred_element_type=F32)

    def finish(s, acc):
        rows = slice(s * QKV_SUB, (s + 1) * QKV_SUB)
        vt_ref[0, :, rows] = acc[:, QK_WIDTH:].T.astype(BF16)
        cosv = cos_ref[rows, :]
        sinv = sin_ref[rows, :]
        for j in range(QK_WIDTH // MXU_COLS):
            y = acc[:, j * MXU_COLS:(j + 1) * MXU_COLS]
            ss = jnp.dot((y * y).astype(BF16), gsum_ref[...], preferred_element_type=F32)
            yn = (y * lax.rsqrt(ss * (1.0 / HEAD_DIM) + EPS)
                  * gains[:, j * MXU_COLS:(j + 1) * MXU_COLS])
            for t in range(MXU_COLS // LANES):
                ys = yn[:, t * LANES:(t + 1) * LANES]
                partner = jnp.where(even_lane,
                                    pltpu.roll(ys, LANES - 1, 1),
                                    pltpu.roll(ys, 1, 1))
                r = ys * cosv + partner * sinv
                col = j * MXU_COLS + t * LANES
                if col < Q_WIDTH:
                    qt_ref[0, col:col + LANES, rows] = r.T.astype(BF16)
                else:
                    k_ref[0, rows, col - Q_WIDTH:col - Q_WIDTH + LANES] = r.astype(BF16)

    pending = {s: project(s) for s in range(min(QKV_AHEAD, n_sub))}
    for s in range(n_sub):
        if s + QKV_AHEAD < n_sub:
            pending[s + QKV_AHEAD] = project(s + QKV_AHEAD)
        finish(s, pending.pop(s))


def _qkv_call(x, mod, layer, norm_g, w_qkv, attn_layer, gsum, qk_gain, cos_t, sin_t):
    batch, seq, _ = x.shape
    tm = QKV_TM
    return pl.pallas_call(
        _qkv_kernel,
        grid=(batch, seq // tm),
        in_specs=[
            pl.BlockSpec((1, tm, D_MODEL), lambda b, i: (b, i, 0)),
            _mod_spec(layer),
            _layer_slab(norm_g, layer),
            _layer_slab(w_qkv, attn_layer),
            _resident((MXU_COLS, MXU_COLS), lambda b, i: (0, 0)),
            _resident((1, QK_WIDTH), lambda b, i: (0, 0)),
            pl.BlockSpec((tm, LANES), lambda b, i: (i, 0)),
            pl.BlockSpec((tm, LANES), lambda b, i: (i, 0)),
        ],
        out_specs=[
            pl.BlockSpec((1, Q_WIDTH, tm), lambda b, i: (b, 0, i)),
            pl.BlockSpec((1, tm, KV_WIDTH), lambda b, i: (b, i, 0)),
            pl.BlockSpec((1, KV_WIDTH, tm), lambda b, i: (b, 0, i)),
        ],
        out_shape=[
            jax.ShapeDtypeStruct((batch, Q_WIDTH, seq), BF16),
            jax.ShapeDtypeStruct((batch, seq, KV_WIDTH), BF16),
            jax.ShapeDtypeStruct((batch, KV_WIDTH, seq), BF16),
        ],
        compiler_params=_cparams(2),
        name="attn_qkv",
    )(x, mod, norm_g, w_qkv, gsum, qk_gain, cos_t, sin_t)


ATTN_TQ = 512
ATTN_TILES = 2
ATTN_KC = 256
ATTN_SLOTS = 3
ONES_ROWS = 16
KV_PER_STEP = N_KV_HEADS
KV_PER_LANES = LANES // HEAD_DIM


def _attn_kernel(qt_ref, k_ref, vt_ref, o_ref, *st_scr):
    n_slots = len(st_scr)
    ahead = n_slots - 1
    seq = k_ref.shape[1]
    tq = ATTN_TQ
    n_heads = KV_PER_STEP * KV_GROUP
    n_units = n_heads * (qt_ref.shape[2] // tq)
    n_kc = seq // ATTN_KC
    zeros = jnp.zeros((HEAD_DIM, tq), BF16)
    ones = jnp.ones((ONES_ROWS, ATTN_KC), BF16)

    def q_operand(u):
        t, h = divmod(u, n_heads)
        qt = qt_ref[0, h * HEAD_DIM:(h + 1) * HEAD_DIM, t * tq:(t + 1) * tq]
        first = (h // KV_GROUP) % KV_PER_LANES == 0
        return jnp.concatenate([qt, zeros] if first else [zeros, qt], axis=0)

    def score_chunk(u, qt_pad, c):
        rows = slice(c * ATTN_KC, (c + 1) * ATTN_KC)
        blk = ((u % n_heads) // KV_GROUP) // KV_PER_LANES
        k_blk = k_ref[0, rows, blk * LANES:(blk + 1) * LANES]
        st = jnp.dot(k_blk, qt_pad, preferred_element_type=F32)
        st_scr[u % n_slots][rows, :] = st
        return jnp.max(st, axis=0, keepdims=True)

    def reduce_max(parts):
        while len(parts) > 1:
            parts = [jnp.maximum(a, b) for a, b in zip(parts[0::2], parts[1::2])]
        return parts[0]

    maxes = {}
    for u in range(min(ahead, n_units)):
        qt_pad = q_operand(u)
        maxes[u] = reduce_max([score_chunk(u, qt_pad, c) for c in range(n_kc)])
    done = []
    for u in range(n_units):
        t, h = divmod(u, n_heads)
        m = maxes.pop(u)
        g = h // KV_GROUP
        nxt = u + ahead
        if nxt < n_units:
            qt_pad = q_operand(nxt)
        parts = []
        acc = jnp.zeros((HEAD_DIM + ONES_ROWS, tq), F32)
        for c in range(n_kc):
            rows = slice(c * ATTN_KC, (c + 1) * ATTN_KC)
            if nxt < n_units:
                parts.append(score_chunk(nxt, qt_pad, c))
            pt = jnp.exp2(st_scr[u % n_slots][rows, :] - m).astype(BF16)
            vt_aug = jnp.concatenate([vt_ref[0, g * HEAD_DIM:(g + 1) * HEAD_DIM, rows], ones],
                                     axis=0)
            acc = acc + jnp.dot(vt_aug, pt, preferred_element_type=F32)
        if parts:
            maxes[nxt] = reduce_max(parts)
        done.append(acc[:HEAD_DIM] / acc[HEAD_DIM:HEAD_DIM + 1])
        if len(done) == 2:
            both = jnp.concatenate(done, axis=0)
            o_ref[0, t * tq:(t + 1) * tq, (h - 1) * HEAD_DIM:(h + 1) * HEAD_DIM] = (
                both.T.astype(BF16))
            done = []


def _attn_call(qt, k, vt):
    batch, seq, _ = k.shape
    tq = ATTN_TQ * ATTN_TILES
    qw = KV_PER_STEP * KV_GROUP * HEAD_DIM
    return pl.pallas_call(
        _attn_kernel,
        grid=(batch, N_KV_HEADS // KV_PER_STEP, seq // tq),
        in_specs=[
            pl.BlockSpec((1, qw, tq), lambda b, j, i: (b, j, i)),
            pl.BlockSpec((1, seq, KV_PER_STEP * HEAD_DIM), lambda b, j, i: (b, 0, j)),
            pl.BlockSpec((1, KV_PER_STEP * HEAD_DIM, seq), lambda b, j, i: (b, j, 0)),
        ],
        out_specs=pl.BlockSpec((1, tq, qw), lambda b, j, i: (b, i, j)),
        out_shape=jax.ShapeDtypeStruct((batch, seq, Q_WIDTH), BF16),
        scratch_shapes=[pltpu.VMEM((seq, ATTN_TQ), F32)] * ATTN_SLOTS,
        compiler_params=_cparams(3),
        name="attn_core",
    )(qt, k, vt)


MLP_TM = 1024
MLP_TH = 1024


def _mlp_body(xv, mod, g_ref, win_ref, wout_ref, out_ref, a_scr):
    shift = mod[:, 3 * D_MODEL:4 * D_MODEL]
    scale = mod[:, 4 * D_MODEL:5 * D_MODEL]
    gate = mod[:, 5 * D_MODEL:6 * D_MODEL]
    h = _modulated_rmsnorm(xv, g_ref[...], shift, scale).astype(BF16)
    for c in range(MLP_HIDDEN // MLP_TH):
        a = jnp.dot(h, win_ref[:, c * MLP_TH:(c + 1) * MLP_TH], preferred_element_type=F32)
        a = jnp.maximum(a, 0.0)
        a_scr[:, c * MLP_TH:(c + 1) * MLP_TH] = (a * a).astype(BF16)
    y = jnp.dot(a_scr[...], wout_ref[...], preferred_element_type=F32)
    out_ref[0] = xv + gate * y


def _mlp_kernel(x_ref, mod_ref, g_ref, win_ref, wout_ref, out_ref, a_scr):
    _mlp_body(x_ref[0], mod_ref[0], g_ref, win_ref, wout_ref, out_ref, a_scr)


def _proj_mlp_kernel(x_ref, o_ref, wo_ref, mod_ref, g_ref, win_ref, wout_ref, out_ref, a_scr):
    mod = mod_ref[0]
    gate1 = mod[:, 2 * D_MODEL:3 * D_MODEL]
    y = jnp.dot(o_ref[0], wo_ref[...], preferred_element_type=F32)
    _mlp_body(x_ref[0] + gate1 * y, mod, g_ref, win_ref, wout_ref, out_ref, a_scr)


def _mlp_call(x, mod, layer, norm_g, w_in, w_out, attn=None):
    batch, seq, _ = x.shape
    tm = MLP_TM
    tile = pl.BlockSpec((1, tm, D_MODEL), lambda b, i: (b, i, 0))
    specs = [_mod_spec(layer), _layer_slab(norm_g, layer), _layer_slab(w_in, layer),
             _layer_slab(w_out, layer)]
    args = (mod, norm_g, w_in, w_out)
    if attn is None:
        body, pre_specs, pre_args = _mlp_kernel, [tile], (x,)
    else:
        o, w_o, attn_layer = attn
        body = _proj_mlp_kernel
        pre_specs = [tile, pl.BlockSpec((1, tm, Q_WIDTH), lambda b, i: (b, i, 0)),
                     _layer_slab(w_o, attn_layer)]
        pre_args = (x, o, w_o)
    return pl.pallas_call(
        body,
        grid=(batch, seq // tm),
        in_specs=pre_specs + specs,
        out_specs=tile,
        out_shape=jax.ShapeDtypeStruct(x.shape, F32),
        scratch_shapes=[pltpu.VMEM((tm, MLP_HIDDEN), BF16)],
        compiler_params=_cparams(2),
        name="sq_relu_mlp" if attn is None else "proj_sq_relu_mlp",
    )(*pre_args, *args)


GMLP_TM = 512
GMLP_TN = 512
PAIR_WIDTH = 2 * GMLP_GROUP_WIDTH


def _gelu(z):
    return 0.5 * z * (1.0 + lax.erf(z * (2.0 ** -0.5)))


def _gmlp_kernel(x_ref, mod_ref, g_ref, win_ref, bin_ref, lng_ref, lnb_ref, ws_ref, bs_ref,
                 wout_ref, out_ref, v_scr, u_scr, vn_scr, uv_scr):
    mod = mod_ref[0]
    shift = mod[:, 0:D_MODEL]
    scale = mod[:, D_MODEL:2 * D_MODEL]
    gate = mod[:, 2 * D_MODEL:3 * D_MODEL]
    xv = x_ref[0]
    tm = xv.shape[0]
    h = _modulated_rmsnorm(xv, g_ref[...], shift, scale).astype(BF16)
    n_col = GMLP_HALF // GMLP_TN

    s1 = jnp.zeros((tm, 1), F32)
    s2 = jnp.zeros((tm, 1), F32)
    pivot = None
    for c in range(n_col):
        lo = GMLP_HALF + c * GMLP_TN
        z = jnp.dot(h, win_ref[:, lo:lo + GMLP_TN], preferred_element_type=F32)
        zg = _gelu(z + bin_ref[:, lo:lo + GMLP_TN])
        v_scr[:, c * GMLP_TN:(c + 1) * GMLP_TN] = zg
        if pivot is None:
            pivot = jnp.sum(zg, axis=-1, keepdims=True) * (1.0 / GMLP_TN)
        d = zg - pivot
        s1 += jnp.sum(d, axis=-1, keepdims=True)
        s2 += jnp.sum(d * d, axis=-1, keepdims=True)
    for c in range(n_col):
        sl = slice(c * GMLP_TN, (c + 1) * GMLP_TN)
        z = jnp.dot(h, win_ref[:, sl], preferred_element_type=F32)
        u_scr[:, sl] = _gelu(z + bin_ref[:, sl])
    off = s1 * (1.0 / GMLP_HALF)
    mu = pivot + off
    rstd = lax.rsqrt(s2 * (1.0 / GMLP_HALF) - off * off + EPS)
    lane = lax.broadcasted_iota(jnp.int32, (GMLP_CHUNK, LANES), 1)
    low_lanes = lane < (GMLP_GROUP_WIDTH - LANES)
    n_rc = tm // GMLP_CHUNK
    for n in range(n_rc):
        rows = slice(n * GMLP_CHUNK, (n + 1) * GMLP_CHUNK)
        vn = (v_scr[rows, :] - mu[rows]) * rstd[rows] * lng_ref[...] + lnb_ref[...]
        vn_scr[rows, :] = vn.astype(BF16)
        for j in range(GMLP_GROUPS // 2):
            base = j * PAIR_WIDTH
            va = vn_scr[rows, base:base + LANES]
            vc = vn_scr[rows, base + LANES:base + 2 * LANES]
            vb = vn_scr[rows, base + 2 * LANES:base + 3 * LANES]
            ra = jnp.dot(ws_ref[2 * j], jnp.concatenate([va, vc], axis=1),
                         preferred_element_type=F32)
            rb = jnp.dot(ws_ref[2 * j + 1], jnp.concatenate([vb, vc], axis=1),
                         preferred_element_type=F32)
            rc = jnp.where(low_lanes, ra[:, LANES:], rb[:, LANES:])
            v_scr[rows, base:base + LANES] = ra[:, :LANES] + bs_ref[:, base:base + LANES]
            v_scr[rows, base + LANES:base + 2 * LANES] = (
                rc + bs_ref[:, base + LANES:base + 2 * LANES])
            v_scr[rows, base + 2 * LANES:base + 3 * LANES] = (
                rb[:, :LANES] + bs_ref[:, base + 2 * LANES:base + 3 * LANES])
        uv_scr[rows, :] = (u_scr[rows, :] * v_scr[rows, :]).astype(BF16)

    y = jnp.dot(uv_scr[...], wout_ref[...], preferred_element_type=F32)
    out_ref[0] = xv + gate * y


def _gmlp_call(x, mod, layer, norm_g, gmlp_layer, w_in, b_in, ln_g, ln_b, w_s, bs_table, w_out):
    batch, seq, _ = x.shape
    tm = GMLP_TM
    gmlp_params = (w_in, b_in, ln_g, ln_b, w_s, bs_table, w_out)
    return pl.pallas_call(
        _gmlp_kernel,
        grid=(batch, seq // tm),
        in_specs=[
            pl.BlockSpec((1, tm, D_MODEL), lambda b, i: (b, i, 0)),
            _mod_spec(layer),
            _layer_slab(norm_g, layer),
        ] + [_layer_slab(p, gmlp_layer) for p in gmlp_params],
        out_specs=pl.BlockSpec((1, tm, D_MODEL), lambda b, i: (b, i, 0)),
        out_shape=jax.ShapeDtypeStruct(x.shape, F32),
        scratch_shapes=[
            pltpu.VMEM((tm, GMLP_HALF), F32),
            pltpu.VMEM((tm, GMLP_HALF), F32),
            pltpu.VMEM((tm, GMLP_HALF), BF16),
            pltpu.VMEM((tm, GMLP_HALF), BF16),
        ],
        compiler_params=_cparams(2),
        name="gmlp_mixer",
    )(x, mod, norm_g, *gmlp_params)


def _rope_tables(seq_len):
    t = jnp.arange(seq_len, dtype=jnp.int32)
    rows = seq_len // GRID_W
    row = (t // GRID_W - rows // 2).astype(F32)
    col = (t % GRID_W - GRID_W // 2).astype(F32)
    inv_freq = ROPE_THETA ** (-jnp.arange(ROPE_PAIRS_AXIS, dtype=F32) / ROPE_PAIRS_AXIS)
    ang = jnp.concatenate([row[:, None] * inv_freq, col[:, None] * inv_freq], axis=-1)
    reps = LANES // HEAD_DIM
    cos_t = jnp.tile(jnp.repeat(jnp.cos(ang), 2, axis=-1), (1, reps))
    sin_t = jnp.tile(jnp.repeat(jnp.sin(ang), 2, axis=-1), (1, reps))
    sign = jnp.tile(jnp.asarray([-1.0, 1.0], F32), LANES // 2)
    return cos_t, sin_t * sign


def kernel(x, c, ada_w, ada_b, norm1_g, norm2_g, attn_w_qkv, attn_q_norm_g, attn_k_norm_g,
           attn_w_o, gmlp_w_in, gmlp_b_in, gmlp_ln_g, gmlp_ln_b, gmlp_w_s, gmlp_b_s,
           gmlp_w_out, mlp_w_in, mlp_w_out):
    batch, seq, _ = x.shape
    mod = _mod_call(c, ada_w, ada_b)
    cos_t, sin_t = _rope_tables(seq)
    head_ids = jnp.arange(MXU_COLS) // HEAD_DIM
    gsum = (head_ids[:, None] == head_ids[None, :]).astype(BF16)
    q_scale = HEAD_DIM ** -0.5 * math.log2(math.e)

    mod = mod.reshape(DEPTH, batch, 1, N_MOD * D_MODEL)
    mlp_w_in, mlp_w_out = mlp_w_in.astype(BF16), mlp_w_out.astype(BF16)
    attn_w_qkv, attn_w_o = attn_w_qkv.astype(BF16), attn_w_o.astype(BF16)
    gmlp_w_in, gmlp_w_out = gmlp_w_in.astype(BF16), gmlp_w_out.astype(BF16)
    gmlp_w_s = gmlp_w_s.astype(BF16)
    n1 = norm1_g[:, None, :]
    n2 = norm2_g[:, None, :]
    b_in = gmlp_b_in[:, None, :]
    ln_g = gmlp_ln_g[:, None, :]
    ln_b = gmlp_ln_b[:, None, :]
    bs_table = jnp.repeat(jnp.swapaxes(gmlp_b_s, 1, 2), GMLP_GROUP_WIDTH, axis=2)

    for i in range(DEPTH):
        j = i // 2
        if i % 2 == 0:
            qk_gain = jnp.concatenate([jnp.tile(attn_q_norm_g[j] * q_scale, N_HEADS),
                                       jnp.tile(attn_k_norm_g[j], N_KV_HEADS)])
            qt, k, vt = _qkv_call(x, mod, i, n1, attn_w_qkv, j, gsum,
                                  qk_gain.reshape(1, QK_WIDTH), cos_t, sin_t)
            o = _attn_call(qt, k, vt)
            x = _mlp_call(x, mod, i, n2, mlp_w_in, mlp_w_out, attn=(o, attn_w_o, j))
        else:
            x = _gmlp_call(x, mod, i, n1, j, gmlp_w_in, b_in, ln_g, ln_b, gmlp_w_s, bs_table,
                           gmlp_w_out)
            x = _mlp_call(x, mod, i, n2, mlp_w_in, mlp_w_out)
    return x
```

```python
import math

import jax
import jax.numpy as jnp
from jax import lax
from jax.experimental import pallas as pl
from jax.experimental.pallas import tpu as pltpu

D_MODEL = 1024
DEPTH = 4
HEAD_DIM = 64
N_HEADS = 16
N_KV_HEADS = 4
KV_GROUP = N_HEADS // N_KV_HEADS
Q_WIDTH = N_HEADS * HEAD_DIM
KV_WIDTH = N_KV_HEADS * HEAD_DIM
QK_WIDTH = Q_WIDTH + KV_WIDTH
QKV_WIDTH = Q_WIDTH + 2 * KV_WIDTH
ROPE_THETA = 10000.0
ROPE_PAIRS_AXIS = HEAD_DIM // 4
GRID_W = 64
GMLP_FFN = 6 * D_MODEL
GMLP_HALF = GMLP_FFN // 2
GMLP_CHUNK = 128
GMLP_GROUPS = 16
GMLP_GROUP_WIDTH = GMLP_HALF // GMLP_GROUPS
MLP_HIDDEN = 4 * D_MODEL
N_MOD = 6
EPS = 1e-6

LANES = 128
MXU_COLS = 256
VMEM_LIMIT = 56 * 1024 * 1024

F32 = jnp.float32
BF16 = jnp.bfloat16


def _cparams(n_axes):
    return pltpu.CompilerParams(
        dimension_semantics=("arbitrary",) * n_axes,
        vmem_limit_bytes=VMEM_LIMIT,
    )


def _resident(block_shape, index_map):
    return pl.BlockSpec(block_shape, index_map, pipeline_mode=pl.Buffered(1))


def _layer_slab(stacked, layer):
    tail = stacked.shape[1:]
    return _resident((None,) + tail, lambda b, i: (layer,) + (0,) * len(tail))


def _mod_spec(layer):
    return pl.BlockSpec((None, 1, 1, N_MOD * D_MODEL), lambda b, i: (layer, b, 0, 0))


def _split_bf16(a):
    hi = a.astype(BF16)
    lo = (a - hi.astype(F32)).astype(BF16)
    return hi, lo


def _modulated_rmsnorm(xv, gain, shift, scale):
    ms = jnp.mean(xv * xv, axis=-1, keepdims=True)
    y = xv * lax.rsqrt(ms + EPS) * gain
    return y * (1.0 + scale) + shift


MOD_TN = 3072


def _mod_kernel(c_ref, w_ref, b_ref, o_ref):
    cv = c_ref[...]
    cond = cv * jax.nn.sigmoid(cv)
    c_hi, c_lo = _split_bf16(cond)
    w_hi, w_lo = _split_bf16(w_ref[0])
    acc = jnp.dot(c_hi, w_hi, preferred_element_type=F32)
    acc += jnp.dot(c_lo, w_hi, preferred_element_type=F32)
    acc += jnp.dot(c_hi, w_lo, preferred_element_type=F32)
    o_ref[0] = acc + b_ref[0]


def _mod_call(c, ada_w, ada_b):
    batch = c.shape[0]
    n_out = N_MOD * D_MODEL
    return pl.pallas_call(
        _mod_kernel,
        grid=(DEPTH, n_out // MOD_TN),
        in_specs=[
            pl.BlockSpec((batch, D_MODEL), lambda l, j: (0, 0)),
            pl.BlockSpec((1, D_MODEL, MOD_TN), lambda l, j: (l, 0, j)),
            pl.BlockSpec((1, 1, MOD_TN), lambda l, j: (l, 0, j)),
        ],
        out_specs=pl.BlockSpec((1, batch, MOD_TN), lambda l, j: (l, 0, j)),
        out_shape=jax.ShapeDtypeStruct((DEPTH, batch, n_out), F32),
        compiler_params=_cparams(2),
        name="adaln_mod",
    )(c, ada_w, ada_b.reshape(DEPTH, 1, n_out))


QKV_TM = 2048
QKV_SUB = 256
QKV_AHEAD = 1


def _qkv_kernel(x_ref, mod_ref, g_ref, w_ref, gsum_ref, qkg_ref, cos_ref, sin_ref,
                qt_ref, k_ref, vt_ref):
    mod = mod_ref[0]
    shift = mod[:, 0:D_MODEL]
    scale = mod[:, D_MODEL:2 * D_MODEL]
    gains = qkg_ref[...]
    lane = lax.broadcasted_iota(jnp.int32, (QKV_SUB, LANES), 1)
    even_lane = (lane & 1) == 0
    n_sub = x_ref.shape[1] // QKV_SUB

    def project(s):
        rows = slice(s * QKV_SUB, (s + 1) * QKV_SUB)
        h = _modulated_rmsnorm(x_ref[0, rows, :], g_ref[...], shift, scale).astype(BF16)
        return jnp.dot(h, w_ref[...], preferred_element_type=F32)

    def finish(s, acc):
        rows = slice(s * QKV_SUB, (s + 1) * QKV_SUB)
        vt_ref[0, :, rows] = acc[:, QK_WIDTH:].T.astype(BF16)
        cosv = cos_ref[rows, :]
        sinv = sin_ref[rows, :]
        for j in range(QK_WIDTH // MXU_COLS):
            y = acc[:, j * MXU_COLS:(j + 1) * MXU_COLS]
            ss = jnp.dot((y * y).astype(BF16), gsum_ref[...], preferred_element_type=F32)
            yn = (y * lax.rsqrt(ss * (1.0 / HEAD_DIM) + EPS)
                  * gains[:, j * MXU_COLS:(j + 1) * MXU_COLS])
            for t in range(MXU_COLS // LANES):
                ys = yn[:, t * LANES:(t + 1) * LANES]
                partner = jnp.where(even_lane,
                                    pltpu.roll(ys, LANES - 1, 1),
                                    pltpu.roll(ys, 1, 1))
                r = ys * cosv + partner * sinv
                col = j * MXU_COLS + t * LANES
                if col < Q_WIDTH:
                    qt_ref[0, col:col + LANES, rows] = r.T.astype(BF16)
                else:
                    k_ref[0, rows, col - Q_WIDTH:col - Q_WIDTH + LANES] = r.astype(BF16)

    pending = {s: project(s) for s in range(min(QKV_AHEAD, n_sub))}
    for s in range(n_sub):
        if s + QKV_AHEAD < n_sub:
            pending[s + QKV_AHEAD] = project(s + QKV_AHEAD)
        finish(s, pending.pop(s))


def _qkv_call(x, mod, layer, norm_g, w_qkv, attn_layer, gsum, qk_gain, cos_t, sin_t):
    batch, seq, _ = x.shape
    tm = QKV_TM
    return pl.pallas_call(
        _qkv_kernel,
        grid=(batch, seq // tm),
        in_specs=[
            pl.BlockSpec((1, tm, D_MODEL), lambda b, i: (b, i, 0)),
            _mod_spec(layer),
            _layer_slab(norm_g, layer),
            _layer_slab(w_qkv, attn_layer),
            _resident((MXU_COLS, MXU_COLS), lambda b, i: (0, 0)),
            _resident((1, QK_WIDTH), lambda b, i: (0, 0)),
            pl.BlockSpec((tm, LANES), lambda b, i: (i, 0)),
            pl.BlockSpec((tm, LANES), lambda b, i: (i, 0)),
        ],
        out_specs=[
            pl.BlockSpec((1, Q_WIDTH, tm), lambda b, i: (b, 0, i)),
            pl.BlockSpec((1, tm, KV_WIDTH), lambda b, i: (b, i, 0)),
            pl.BlockSpec((1, KV_WIDTH, tm), lambda b, i: (b, 0, i)),
        ],
        out_shape=[
            jax.ShapeDtypeStruct((batch, Q_WIDTH, seq), BF16),
            jax.ShapeDtypeStruct((batch, seq, KV_WIDTH), BF16),
            jax.ShapeDtypeStruct((batch, KV_WIDTH, seq), BF16),
        ],
        compiler_params=_cparams(2),
        name="attn_qkv",
    )(x, mod, norm_g, w_qkv, gsum, qk_gain, cos_t, sin_t)


ATTN_TQ = 512
ATTN_TILES = 2
ATTN_KC = 256
ATTN_SLOTS = 3
ONES_ROWS = 16
KV_PER_STEP = N_KV_HEADS
KV_PER_LANES = LANES // HEAD_DIM


def _attn_kernel(qt_ref, k_ref, vt_ref, o_ref, *st_scr):
    n_slots = len(st_scr)
    ahead = n_slots - 1
    seq = k_ref.shape[1]
    tq = ATTN_TQ
    n_heads = KV_PER_STEP * KV_GROUP
    n_units = n_heads * (qt_ref.shape[2] // tq)
    n_kc = seq // ATTN_KC
    zeros = jnp.zeros((HEAD_DIM, tq), BF16)
    ones = jnp.ones((ONES_ROWS, ATTN_KC), BF16)

    def q_operand(u):
        t, h = divmod(u, n_heads)
        qt = qt_ref[0, h * HEAD_DIM:(h + 1) * HEAD_DIM, t * tq:(t + 1) * tq]
        first = (h // KV_GROUP) % KV_PER_LANES == 0
        return jnp.concatenate([qt, zeros] if first else [zeros, qt], axis=0)

    def score_chunk(u, qt_pad, c):
        rows = slice(c * ATTN_KC, (c + 1) * ATTN_KC)
        blk = ((u % n_heads) // KV_GROUP) // KV_PER_LANES
        k_blk = k_ref[0, rows, blk * LANES:(blk + 1) * LANES]
        st = jnp.dot(k_blk, qt_pad, preferred_element_type=F32)
        st_scr[u % n_slots][rows, :] = st
        return jnp.max(st, axis=0, keepdims=True)

    def reduce_max(parts):
        while len(parts) > 1:
            parts = [jnp.maximum(a, b) for a, b in zip(parts[0::2], parts[1::2])]
        return parts[0]

    maxes = {}
    for u in range(min(ahead, n_units)):
        qt_pad = q_operand(u)
        maxes[u] = reduce_max([score_chunk(u, qt_pad, c) for c in range(n_kc)])
    done = []
    for u in range(n_units):
        t, h = divmod(u, n_heads)
        m = maxes.pop(u)
        g = h // KV_GROUP
        nxt = u + ahead
        if nxt < n_units:
            qt_pad = q_operand(nxt)
        parts = []
        acc = jnp.zeros((HEAD_DIM + ONES_ROWS, tq), F32)
        for c in range(n_kc):
            rows = slice(c * ATTN_KC, (c + 1) * ATTN_KC)
            if nxt < n_units:
                parts.append(score_chunk(nxt, qt_pad, c))
            pt = jnp.exp2(st_scr[u % n_slots][rows, :] - m).astype(BF16)
            vt_aug = jnp.concatenate([vt_ref[0, g * HEAD_DIM:(g + 1) * HEAD_DIM, rows], ones],
                                     axis=0)
            acc = acc + jnp.dot(vt_aug, pt, preferred_element_type=F32)
        if parts:
            maxes[nxt] = reduce_max(parts)
        done.append(acc[:HEAD_DIM] / acc[HEAD_DIM:HEAD_DIM + 1])
        if len(done) == 2:
            both = jnp.concatenate(done, axis=0)
            o_ref[0, (h - 1) * HEAD_DIM:(h + 1) * HEAD_DIM, t * tq:(t + 1) * tq] = (
                both.astype(BF16))
            done = []


def _attn_call(qt, k, vt):
    batch, seq, _ = k.shape
    tq = ATTN_TQ * ATTN_TILES
    qw = KV_PER_STEP * KV_GROUP * HEAD_DIM
    return pl.pallas_call(
        _attn_kernel,
        grid=(batch, N_KV_HEADS // KV_PER_STEP, seq // tq),
        in_specs=[
            pl.BlockSpec((1, qw, tq), lambda b, j, i: (b, j, i)),
            pl.BlockSpec((1, seq, KV_PER_STEP * HEAD_DIM), lambda b, j, i: (b, 0, j)),
            pl.BlockSpec((1, KV_PER_STEP * HEAD_DIM, seq), lambda b, j, i: (b, j, 0)),
        ],
        out_specs=pl.BlockSpec((1, qw, tq), lambda b, j, i: (b, j, i)),
        out_shape=jax.ShapeDtypeStruct((batch, Q_WIDTH, seq), BF16),
        scratch_shapes=[pltpu.VMEM((seq, ATTN_TQ), F32)] * ATTN_SLOTS,
        compiler_params=_cparams(3),
        name="attn_core",
    )(qt, k, vt)


MLP_TM = 1024
MLP_TH = 1024


def _mlp_body(xv, mod, g_ref, win_ref, wout_ref, out_ref, a_scr):
    shift = mod[:, 3 * D_MODEL:4 * D_MODEL]
    scale = mod[:, 4 * D_MODEL:5 * D_MODEL]
    gate = mod[:, 5 * D_MODEL:6 * D_MODEL]
    h = _modulated_rmsnorm(xv, g_ref[...], shift, scale).astype(BF16)
    for c in range(MLP_HIDDEN // MLP_TH):
        a = jnp.dot(h, win_ref[:, c * MLP_TH:(c + 1) * MLP_TH], preferred_element_type=F32)
        a = jnp.maximum(a, 0.0)
        a_scr[:, c * MLP_TH:(c + 1) * MLP_TH] = (a * a).astype(BF16)
    y = jnp.dot(a_scr[...], wout_ref[...], preferred_element_type=F32)
    out_ref[0] = xv + gate * y


def _mlp_kernel(x_ref, mod_ref, g_ref, win_ref, wout_ref, out_ref, a_scr):
    _mlp_body(x_ref[0], mod_ref[0], g_ref, win_ref, wout_ref, out_ref, a_scr)


def _proj_mlp_kernel(x_ref, o_ref, wo_ref, mod_ref, g_ref, win_ref, wout_ref, out_ref, a_scr):
    mod = mod_ref[0]
    gate1 = mod[:, 2 * D_MODEL:3 * D_MODEL]
    y = lax.dot_general(o_ref[0], wo_ref[...], (((0,), (0,)), ((), ())),
                        preferred_element_type=F32)
    _mlp_body(x_ref[0] + gate1 * y, mod, g_ref, win_ref, wout_ref, out_ref, a_scr)


def _mlp_call(x, mod, layer, norm_g, w_in, w_out, attn=None):
    batch, seq, _ = x.shape
    tm = MLP_TM
    tile = pl.BlockSpec((1, tm, D_MODEL), lambda b, i: (b, i, 0))
    specs = [_mod_spec(layer), _layer_slab(norm_g, layer), _layer_slab(w_in, layer),
             _layer_slab(w_out, layer)]
    args = (mod, norm_g, w_in, w_out)
    if attn is None:
        body, pre_specs, pre_args = _mlp_kernel, [tile], (x,)
    else:
        o, w_o, attn_layer = attn
        body = _proj_mlp_kernel
        pre_specs = [tile, pl.BlockSpec((1, Q_WIDTH, tm), lambda b, i: (b, 0, i)),
                     _layer_slab(w_o, attn_layer)]
        pre_args = (x, o, w_o)
    return pl.pallas_call(
        body,
        grid=(batch, seq // tm),
        in_specs=pre_specs + specs,
        out_specs=tile,
        out_shape=jax.ShapeDtypeStruct(x.shape, F32),
        scratch_shapes=[pltpu.VMEM((tm, MLP_HIDDEN), BF16)],
        compiler_params=_cparams(2),
        name="sq_relu_mlp" if attn is None else "proj_sq_relu_mlp",
    )(*pre_args, *args)


GMLP_TM = 512
GMLP_TN = 512
PAIR_WIDTH = 2 * GMLP_GROUP_WIDTH


def _gelu(z):
    return 0.5 * z * (1.0 + lax.erf(z * (2.0 ** -0.5)))


def _gmlp_kernel(x_ref, mod_ref, g_ref, win_ref, bin_ref, lng_ref, lnb_ref, ws_ref, bs_ref,
                 wout_ref, out_ref, v_scr, u_scr, vn_scr, uv_scr):
    mod = mod_ref[0]
    shift = mod[:, 0:D_MODEL]
    scale = mod[:, D_MODEL:2 * D_MODEL]
    gate = mod[:, 2 * D_MODEL:3 * D_MODEL]
    xv = x_ref[0]
    tm = xv.shape[0]
    h = _modulated_rmsnorm(xv, g_ref[...], shift, scale).astype(BF16)
    n_col = GMLP_HALF // GMLP_TN

    s1 = jnp.zeros((tm, 1), F32)
    s2 = jnp.zeros((tm, 1), F32)
    pivot = None
    for c in range(n_col):
        lo = GMLP_HALF + c * GMLP_TN
        z = jnp.dot(h, win_ref[:, lo:lo + GMLP_TN], preferred_element_type=F32)
        zg = _gelu(z + bin_ref[:, lo:lo + GMLP_TN])
        v_scr[:, c * GMLP_TN:(c + 1) * GMLP_TN] = zg
        if pivot is None:
            pivot = jnp.sum(zg, axis=-1, keepdims=True) * (1.0 / GMLP_TN)
        d = zg - pivot
        s1 += jnp.sum(d, axis=-1, keepdims=True)
        s2 += jnp.sum(d * d, axis=-1, keepdims=True)
    for c in range(n_col):
        sl = slice(c * GMLP_TN, (c + 1) * GMLP_TN)
        z = jnp.dot(h, win_ref[:, sl], preferred_element_type=F32)
        u_scr[:, sl] = _gelu(z + bin_ref[:, sl])
    off = s1 * (1.0 / GMLP_HALF)
    mu = pivot + off
    rstd = lax.rsqrt(s2 * (1.0 / GMLP_HALF) - off * off + EPS)
    lane = lax.broadcasted_iota(jnp.int32, (GMLP_CHUNK, LANES), 1)
    low_lanes = lane < (GMLP_GROUP_WIDTH - LANES)
    n_rc = tm // GMLP_CHUNK
    for n in range(n_rc):
        rows = slice(n * GMLP_CHUNK, (n + 1) * GMLP_CHUNK)
        vn = (v_scr[rows, :] - mu[rows]) * rstd[rows] * lng_ref[...] + lnb_ref[...]
        vn_scr[rows, :] = vn.astype(BF16)
        for j in range(GMLP_GROUPS // 2):
            base = j * PAIR_WIDTH
            va = vn_scr[rows, base:base + LANES]
            vc = vn_scr[rows, base + LANES:base + 2 * LANES]
            vb = vn_scr[rows, base + 2 * LANES:base + 3 * LANES]
            ra = jnp.dot(ws_ref[2 * j], jnp.concatenate([va, vc], axis=1),
                         preferred_element_type=F32)
            rb = jnp.dot(ws_ref[2 * j + 1], jnp.concatenate([vb, vc], axis=1),
                         preferred_element_type=F32)
            rc = jnp.where(low_lanes, ra[:, LANES:], rb[:, LANES:])
            v_scr[rows, base:base + LANES] = ra[:, :LANES] + bs_ref[:, base:base + LANES]
            v_scr[rows, base + LANES:base + 2 * LANES] = (
                rc + bs_ref[:, base + LANES:base + 2 * LANES])
            v_scr[rows, base + 2 * LANES:base + 3 * LANES] = (
                rb[:, :LANES] + bs_ref[:, base + 2 * LANES:base + 3 * LANES])
        uv_scr[rows, :] = (u_scr[rows, :] * v_scr[rows, :]).astype(BF16)

    y = jnp.dot(uv_scr[...], wout_ref[...], preferred_element_type=F32)
    out_ref[0] = xv + gate * y


def _gmlp_call(x, mod, layer, norm_g, gmlp_layer, w_in, b_in, ln_g, ln_b, w_s, bs_table, w_out):
    batch, seq, _ = x.shape
    tm = GMLP_TM
    gmlp_params = (w_in, b_in, ln_g, ln_b, w_s, bs_table, w_out)
    return pl.pallas_call(
        _gmlp_kernel,
        grid=(batch, seq // tm),
        in_specs=[
            pl.BlockSpec((1, tm, D_MODEL), lambda b, i: (b, i, 0)),
            _mod_spec(layer),
            _layer_slab(norm_g, layer),
        ] + [_layer_slab(p, gmlp_layer) for p in gmlp_params],
        out_specs=pl.BlockSpec((1, tm, D_MODEL), lambda b, i: (b, i, 0)),
        out_shape=jax.ShapeDtypeStruct(x.shape, F32),
        scratch_shapes=[
            pltpu.VMEM((tm, GMLP_HALF), F32),
            pltpu.VMEM((tm, GMLP_HALF), F32),
            pltpu.VMEM((tm, GMLP_HALF), BF16),
            pltpu.VMEM((tm, GMLP_HALF), BF16),
        ],
        compiler_params=_cparams(2),
        name="gmlp_mixer",
    )(x, mod, norm_g, *gmlp_params)


def _rope_tables(seq_len):
    t = jnp.arange(seq_len, dtype=jnp.int32)
    rows = seq_len // GRID_W
    row = (t // GRID_W - rows // 2).astype(F32)
    col = (t % GRID_W - GRID_W // 2).astype(F32)
    inv_freq = ROPE_THETA ** (-jnp.arange(ROPE_PAIRS_AXIS, dtype=F32) / ROPE_PAIRS_AXIS)
    ang = jnp.concatenate([row[:, None] * inv_freq, col[:, None] * inv_freq], axis=-1)
    reps = LANES // HEAD_DIM
    cos_t = jnp.tile(jnp.repeat(jnp.cos(ang), 2, axis=-1), (1, reps))
    sin_t = jnp.tile(jnp.repeat(jnp.sin(ang), 2, axis=-1), (1, reps))
    sign = jnp.tile(jnp.asarray([-1.0, 1.0], F32), LANES // 2)
    return cos_t, sin_t * sign


def kernel(x, c, ada_w, ada_b, norm1_g, norm2_g, attn_w_qkv, attn_q_norm_g, attn_k_norm_g,
           attn_w_o, gmlp_w_in, gmlp_b_in, gmlp_ln_g, gmlp_ln_b, gmlp_w_s, gmlp_b_s,
           gmlp_w_out, mlp_w_in, mlp_w_out):
    batch, seq, _ = x.shape
    mod = _mod_call(c, ada_w, ada_b)
    cos_t, sin_t = _rope_tables(seq)
    head_ids = jnp.arange(MXU_COLS) // HEAD_DIM
    gsum = (head_ids[:, None] == head_ids[None, :]).astype(BF16)
    q_scale = HEAD_DIM ** -0.5 * math.log2(math.e)

    mod = mod.reshape(DEPTH, batch, 1, N_MOD * D_MODEL)
    mlp_w_in, mlp_w_out = mlp_w_in.astype(BF16), mlp_w_out.astype(BF16)
    attn_w_qkv, attn_w_o = attn_w_qkv.astype(BF16), attn_w_o.astype(BF16)
    gmlp_w_in, gmlp_w_out = gmlp_w_in.astype(BF16), gmlp_w_out.astype(BF16)
    gmlp_w_s = gmlp_w_s.astype(BF16)
    n1 = norm1_g[:, None, :]
    n2 = norm2_g[:, None, :]
    b_in = gmlp_b_in[:, None, :]
    ln_g = gmlp_ln_g[:, None, :]
    ln_b = gmlp_ln_b[:, None, :]
    bs_table = jnp.repeat(jnp.swapaxes(gmlp_b_s, 1, 2), GMLP_GROUP_WIDTH, axis=2)

    for i in range(DEPTH):
        j = i // 2
        if i % 2 == 0:
            qk_gain = jnp.concatenate([jnp.tile(attn_q_norm_g[j] * q_scale, N_HEADS),
                                       jnp.tile(attn_k_norm_g[j], N_KV_HEADS)])
            qt, k, vt = _qkv_call(x, mod, i, n1, attn_w_qkv, j, gsum,
                                  qk_gain.reshape(1, QK_WIDTH), cos_t, sin_t)
            o = _attn_call(qt, k, vt)
            x = _mlp_call(x, mod, i, n2, mlp_w_in, mlp_w_out, attn=(o, attn_w_o, j))
        else:
            x = _gmlp_call(x, mod, i, n1, j, gmlp_w_in, b_in, ln_g, ln_b, gmlp_w_s, bs_table,
                           gmlp_w_out)
            x = _mlp_call(x, mod, i, n2, mlp_w_in, mlp_w_out)
    return x
```
